```python
import math
import jax, jax.numpy as jnp
from jax import lax
import numpy as np

D_MODEL = 1024
BATCH = 32
SEQ = 2048
DEPTH = 1

MEM_LEN = 256
EPS = 1e-6
MLA_HEADS = 8
MLA_NOPE = 64
MLA_ROPE = 32
MLA_V = 64
MLA_Q_RANK = 384
MLA_KV_RANK = 256
ROPE_THETA = 10000.0
Q_BLOCK = 128
ML_HEADS = 8
ML_DK = 64
ML_DV = 64
ML_CHUNK = 64
CONV_WIDTH = 5
MEM_HEADS = 4
MEM_HEAD_DIM = 128
MLA_WIDTH = MLA_HEADS * MLA_V
ML_WIDTH = ML_HEADS * ML_DV
MEM_WIDTH = MEM_HEADS * MEM_HEAD_DIM
N_BRANCH = 3
BRANCH_WIDTH = 512
N_ML_GATES = 4 * ML_HEADS
N_IN = N_BRANCH * D_MODEL + MLA_Q_RANK + MLA_KV_RANK + MLA_ROPE + 3 * ML_WIDTH + N_ML_GATES + MEM_WIDTH
D_FF = -(-8 * D_MODEL // (3 * 256)) * 256

kernel_name = "hybrid_mla_mlstm_memory_gated_block"


def rms_norm(x, g):
    xf = x.astype(jnp.float32)
    y = xf * lax.rsqrt(jnp.mean(xf * xf, axis=-1, keepdims=True) + EPS)
    return (y * g.astype(jnp.float32)).astype(x.dtype)


def rope_angles(positions):
    half = MLA_ROPE // 2
    inv = ROPE_THETA ** (-jnp.arange(half, dtype=jnp.float32) / half)
    ang = positions.astype(jnp.float32)[..., None] * inv
    return jnp.cos(ang), jnp.sin(ang)


def apply_rope(x, cos, sin):
    half = x.shape[-1] // 2
    x1, x2 = x[..., :half], x[..., half:]
    out = jnp.concatenate([x1 * cos - x2 * sin, x1 * sin + x2 * cos], axis=-1)
    return out.astype(x.dtype)


def split_columns(z):
    sizes = (N_BRANCH * D_MODEL, MLA_Q_RANK, MLA_KV_RANK, MLA_ROPE,
             ML_WIDTH, ML_WIDTH, ML_WIDTH, N_ML_GATES, MEM_WIDTH)
    idx = np.cumsum(sizes)[:-1].tolist()
    return jnp.split(z, idx, axis=-1)


def mla_attention(c_q, c_kv, k_rope_in, positions, g_q, g_kv, w_uq, w_uk, w_uv):
    B, S, _ = c_q.shape
    cq = rms_norm(c_q, g_q)
    ckv = rms_norm(c_kv, g_kv)
    q = jnp.einsum('bsr,rhd->bshd', cq, w_uq)
    q_nope, q_rope = q[..., :MLA_NOPE], q[..., MLA_NOPE:]
    cos, sin = rope_angles(positions)
    q_rope = apply_rope(q_rope, cos[:, :, None, :], sin[:, :, None, :])
    k_rope = apply_rope(k_rope_in, cos, sin)
    k_nope = jnp.einsum('bsr,rhd->bshd', ckv, w_uk)
    v = jnp.einsum('bsr,rhd->bshd', ckv, w_uv)
    scale = (MLA_NOPE + MLA_ROPE) ** -0.5
    nb = S // Q_BLOCK
    qn_b = q_nope.reshape(B, nb, Q_BLOCK, MLA_HEADS, MLA_NOPE).swapaxes(0, 1)
    qr_b = q_rope.reshape(B, nb, Q_BLOCK, MLA_HEADS, MLA_ROPE).swapaxes(0, 1)

    def block(args):
        qn, qr = args
        s = (jnp.einsum('bqhd,bkhd->bhqk', qn, k_nope)
             + jnp.einsum('bqhd,bkd->bhqk', qr, k_rope))
        p = jax.nn.softmax(s.astype(jnp.float32) * scale, axis=-1).astype(v.dtype)
        return jnp.einsum('bhqk,bkhd->bqhd', p, v)

    o = lax.map(block, (qn_b, qr_b))
    return o.swapaxes(0, 1).reshape(B, S, MLA_WIDTH)


def mlstm_chunkwise(q, k, v, i_pre, f_pre):
    B, H, S, dk = q.shape
    dv = v.shape[-1]
    L = ML_CHUNK
    nc = S // L
    to_chunks = lambda t: jnp.moveaxis(t.reshape(B, H, nc, L, *t.shape[3:]), 2, 0)
    qc, kc, vc = to_chunks(q), to_chunks(k), to_chunks(v)
    lf = to_chunks(jax.nn.log_sigmoid(f_pre))
    ic = to_chunks(i_pre)
    mask = jnp.tril(jnp.ones((L, L), dtype=bool))

    def step(carry, inp):
        C, n, m = carry
        qb, kb, vb, lfb, ib = inp
        b = jnp.cumsum(lfb, axis=-1)
        dmat = jnp.where(mask, b[..., :, None] - b[..., None, :] + ib[..., None, :], -jnp.inf)
        inter = b + m[..., None]
        m_t = jnp.maximum(inter, dmat.max(axis=-1))
        w_inter = jnp.exp(inter - m_t)
        s = jnp.einsum('bhtd,bhsd->bhts', qb, kb) * jnp.exp(dmat - m_t[..., None])
        num = (jnp.einsum('bhts,bhse->bhte', s, vb)
               + w_inter[..., None] * jnp.einsum('bhed,bhtd->bhte', C, qb))
        den = s.sum(axis=-1) + w_inter * jnp.einsum('bhd,bhtd->bht', n, qb)
        h = num / jnp.maximum(jnp.abs(den), jnp.exp(-m_t))[..., None]
        bL = b[..., -1]
        g = bL[..., None] - b + ib
        m_new = jnp.maximum(bL + m, g.max(axis=-1))
        decay = jnp.exp(bL + m - m_new)
        ws = jnp.exp(g - m_new[..., None])
        C_new = decay[..., None, None] * C + jnp.einsum('bhs,bhse,bhsd->bhed', ws, vb, kb)
        n_new = decay[..., None] * n + jnp.einsum('bhs,bhsd->bhd', ws, kb)
        return (C_new, n_new, m_new), h

    init = (jnp.zeros((B, H, dv, dk), jnp.float32),
            jnp.zeros((B, H, dk), jnp.float32),
            jnp.full((B, H), -jnp.inf, jnp.float32))
    _, hs = lax.scan(step, init, (qc, kc, vc, lf, ic))
    return jnp.moveaxis(hs, 0, 2).reshape(B, H, S, dv)


def mlstm_branch(u, v_in, o_pre, gate_pre, conv_w, w_q, w_k, gate_bias, g_head):
    B, S, _ = u.shape
    dtype = u.dtype
    c = lax.conv_general_dilated(u, conv_w[:, None, :], window_strides=(1,), padding='SAME',
                                 dimension_numbers=('NWC', 'WIO', 'NWC'),
                                 feature_group_count=ML_WIDTH)
    c = jax.nn.silu(c).reshape(B, S, ML_HEADS, ML_DK)
    q = jnp.einsum('bshc,hcd->bhsd', c, w_q).astype(jnp.float32)
    k = (jnp.einsum('bshc,hcd->bhsd', c, w_k) * (ML_DK ** -0.5)).astype(jnp.float32)
    v = v_in.reshape(B, S, ML_HEADS, ML_DV).transpose(0, 2, 1, 3).astype(jnp.float32)
    gates = (gate_pre.reshape(B, S, 4, ML_HEADS) + gate_bias).astype(jnp.float32)
    gates = gates.transpose(2, 0, 3, 1)
    flip = lambda t: jnp.flip(t, axis=2)
    q2 = jnp.concatenate([q, flip(q)], axis=1)
    k2 = jnp.concatenate([k, flip(k)], axis=1)
    v2 = jnp.concatenate([v, flip(v)], axis=1)
    i2 = jnp.concatenate([gates[0], flip(gates[2])], axis=1)
    f2 = jnp.concatenate([gates[1], flip(gates[3])], axis=1)
    h2 = mlstm_chunkwise(q2, k2, v2, i2, f2)
    h = h2[:, :ML_HEADS] + flip(h2[:, ML_HEADS:])
    h = h.transpose(0, 2, 1, 3)
    h = h * lax.rsqrt(jnp.mean(h * h, axis=-1, keepdims=True) + EPS)
    h = h.reshape(B, S, ML_WIDTH) * g_head.astype(jnp.float32)
    return (jax.nn.sigmoid(o_pre.astype(jnp.float32)) * h).astype(dtype)


def memory_attention(q_in, mem_n, w_kv):
    B, S, _ = q_in.shape
    M = mem_n.shape[1]
    q = q_in.reshape(B, S, MEM_HEADS, MEM_HEAD_DIM)
    kv = mem_n @ w_kv
    k = kv[..., :MEM_WIDTH].reshape(B, M, MEM_HEADS, MEM_HEAD_DIM)
    v = kv[..., MEM_WIDTH:].reshape(B, M, MEM_HEADS, MEM_HEAD_DIM)
    s = jnp.einsum('bshd,bmhd->bhsm', q, k).astype(jnp.float32) * (MEM_HEAD_DIM ** -0.5)
    p = jax.nn.softmax(s, axis=-1).astype(v.dtype)
    return jnp.einsum('bhsm,bmhd->bshd', p, v).reshape(B, S, MEM_WIDTH)


def hybrid_layer(x, mem, positions, g_mix, w_in, mla_g_q, mla_g_kv, mla_w_uq, mla_w_uk, mla_w_uv,
                 ml_conv_w, ml_w_q, ml_w_k, ml_gate_bias, ml_g_head, mem_g, mem_w_kv,
                 w_branch, w_out, g_ffn, w_ffn_gate, w_ffn_up, w_ffn_down):
    B, S, D = x.shape
    h = rms_norm(x, g_mix)
    z = h @ w_in
    (z_gate, z_cq, z_ckv, z_kr, z_mu, z_mv, z_mo, z_mg, z_memq) = split_columns(z)
    gates = jax.nn.sigmoid(z_gate.astype(jnp.float32)).reshape(B, S, N_BRANCH, D).astype(x.dtype)
    o_mla = mla_attention(z_cq, z_ckv, z_kr, positions, mla_g_q, mla_g_kv, mla_w_uq, mla_w_uk, mla_w_uv)
    o_ml = mlstm_branch(z_mu, z_mv, z_mo, z_mg, ml_conv_w, ml_w_q, ml_w_k, ml_gate_bias, ml_g_head)
    o_mem = memory_attention(z_memq, rms_norm(mem, mem_g), mem_w_kv)
    merged = (gates[:, :, 0] * (o_mla @ w_branch[0])
              + gates[:, :, 1] * (o_ml @ w_branch[1])
              + gates[:, :, 2] * (o_mem @ w_branch[2]))
    x = x + merged @ w_out
    h2 = rms_norm(x, g_ffn)
    ffn = (jax.nn.silu(h2 @ w_ffn_gate) * (h2 @ w_ffn_up)) @ w_ffn_down
    return x + ffn


def setup_inputs(seed: int = 0) -> dict:
    key = jax.random.key(seed)
    ks = jax.random.split(key, 32)
    f32 = jnp.float32
    nrm = lambda k, shape, fan_in: jax.random.normal(k, shape, f32) * (fan_in ** -0.5)
    gain = lambda k, shape: 1.0 + 0.02 * jax.random.normal(k, shape, f32)
    x = jax.random.normal(ks[0], (BATCH, SEQ, D_MODEL), f32)
    mem = jax.random.normal(ks[1], (BATCH, MEM_LEN, D_MODEL), f32)
    offsets = jax.random.randint(ks[2], (BATCH, 1), 0, 4096, dtype=jnp.int32)
    positions = offsets + jnp.arange(SEQ, dtype=jnp.int32)[None, :]
    f_base = jnp.linspace(3.0, 6.0, ML_HEADS, dtype=f32)
    i_base = jnp.zeros((ML_HEADS,), f32)
    gate_base = jnp.stack([i_base, f_base, i_base, f_base])
    ml_gate_bias = gate_base[None] + 0.01 * jax.random.normal(ks[3], (DEPTH, 4, ML_HEADS), f32)
    return {
        "x": x,
        "mem": mem,
        "positions": positions,
        "g_mix": gain(ks[4], (DEPTH, D_MODEL)),
        "w_in": nrm(ks[5], (DEPTH, D_MODEL, N_IN), D_MODEL),
        "mla_g_q": gain(ks[6], (DEPTH, MLA_Q_RANK)),
        "mla_g_kv": gain(ks[7], (DEPTH, MLA_KV_RANK)),
        "mla_w_uq": nrm(ks[8], (DEPTH, MLA_Q_RANK, MLA_HEADS, MLA_NOPE + MLA_ROPE), MLA_Q_RANK),
        "mla_w_uk": nrm(ks[9], (DEPTH, MLA_KV_RANK, MLA_HEADS, MLA_NOPE), MLA_KV_RANK),
        "mla_w_uv": nrm(ks[10], (DEPTH, MLA_KV_RANK, MLA_HEADS, MLA_V), MLA_KV_RANK),
        "ml_conv_w": nrm(ks[11], (DEPTH, CONV_WIDTH, ML_WIDTH), CONV_WIDTH),
        "ml_w_q": nrm(ks[12], (DEPTH, ML_HEADS, ML_DK, ML_DK), ML_DK),
        "ml_w_k": nrm(ks[13], (DEPTH, ML_HEADS, ML_DK, ML_DK), ML_DK),
        "ml_gate_bias": ml_gate_bias,
        "ml_g_head": gain(ks[14], (DEPTH, ML_WIDTH)),
        "mem_g": gain(ks[15], (DEPTH, D_MODEL)),
        "mem_w_kv": nrm(ks[16], (DEPTH, D_MODEL, 2 * MEM_WIDTH), D_MODEL),
        "w_branch": nrm(ks[17], (DEPTH, N_BRANCH, BRANCH_WIDTH, D_MODEL), BRANCH_WIDTH),
        "w_out": nrm(ks[18], (DEPTH, D_MODEL, D_MODEL), D_MODEL),
        "g_ffn": gain(ks[19], (DEPTH, D_MODEL)),
        "w_ffn_gate": nrm(ks[20], (DEPTH, D_MODEL, D_FF), D_MODEL),
        "w_ffn_up": nrm(ks[21], (DEPTH, D_MODEL, D_FF), D_MODEL),
        "w_ffn_down": nrm(ks[22], (DEPTH, D_FF, D_MODEL), D_FF),
        "g_final": gain(ks[23], (D_MODEL,)),
    }


def reference(x, mem, positions, g_mix, w_in, mla_g_q, mla_g_kv, mla_w_uq, mla_w_uk, mla_w_uv,
              ml_conv_w, ml_w_q, ml_w_k, ml_gate_bias, ml_g_head, mem_g, mem_w_kv,
              w_branch, w_out, g_ffn, w_ffn_gate, w_ffn_up, w_ffn_down, g_final):
    for l in range(DEPTH):
        x = hybrid_layer(x, mem, positions, g_mix[l], w_in[l], mla_g_q[l], mla_g_kv[l],
                         mla_w_uq[l], mla_w_uk[l], mla_w_uv[l], ml_conv_w[l], ml_w_q[l],
                         ml_w_k[l], ml_gate_bias[l], ml_g_head[l], mem_g[l], mem_w_kv[l],
                         w_branch[l], w_out[l], g_ffn[l], w_ffn_gate[l], w_ffn_up[l],
                         w_ffn_down[l])
    return rms_norm(x, g_final)
```

```python
import functools
import math

import numpy as np
import jax
import jax.numpy as jnp
from jax import lax
from jax.experimental import pallas as pl
from jax.experimental.pallas import tpu as pltpu

D_MODEL = 1024
MEM_LEN = 256
EPS = 1e-6
MLA_HEADS = 8
MLA_NOPE = 64
MLA_ROPE = 32
MLA_V = 64
MLA_Q_RANK = 384
MLA_KV_RANK = 256
ROPE_THETA = 10000.0
ML_HEADS = 8
ML_DK = 64
ML_DV = 64
ML_WIDTH = ML_HEADS * ML_DV
CONV_WIDTH = 5
MEM_HEADS = 4
MEM_HEAD_DIM = 128
MEM_WIDTH = MEM_HEADS * MEM_HEAD_DIM
N_BRANCH = 3
N_ML_GATES = 4 * ML_HEADS
D_FF = 2816

LANES = 128
ML_CHUNK = 128
CONV_PAD = 8
VMEM_LIMIT = 56 * 1024 * 1024

BF16 = jnp.bfloat16
F32 = jnp.float32

_C_GATE = 0
_C_CQ = _C_GATE + N_BRANCH * D_MODEL
_C_CKV = _C_CQ + MLA_Q_RANK
_C_KRC = _C_CKV + MLA_KV_RANK
_C_KRS = _C_KRC + LANES
_C_MU = _C_KRS + LANES
_C_MV = _C_MU + ML_WIDTH
_C_MO = _C_MV + ML_WIDTH
_C_MEMQ = _C_MO + ML_WIDTH
_C_MGI = _C_MEMQ + MEM_WIDTH
_C_MGF = _C_MGI + LANES
_C_END = _C_MGF + LANES


def _const_spec(shape):
    nd = len(shape)
    return pl.BlockSpec(shape, lambda *_: (0,) * nd, pipeline_mode=pl.Buffered(1))


def _sigmoid(x):
    return 1.0 / (1.0 + jnp.exp(-x))


def _rms(x, g):
    return x * lax.rsqrt(jnp.mean(x * x, axis=-1, keepdims=True) + EPS) * g


def _dot(a, b):
    return jnp.dot(a, b, preferred_element_type=F32)


def _dot_nt(a, b):
    return lax.dot_general(a, b, (((1,), (1,)), ((), ())), preferred_element_type=F32)


def _dot_tn(a, b):
    return lax.dot_general(a, b, (((0,), (0,)), ((), ())), preferred_element_type=F32)


def _inproj_kernel(x_ref, pos_ref, gmix_ref, w_ref, gbias_ref, gq_ref, gkv_ref, wuq_ref, wuk_ref,
                   wuv_ref, vones_ref, invf_ref,
                   gates_ref, q_ref, k_ref, v_ref, mu_ref, mv_ref, mo_ref, mg_ref, memq_ref):
    x = x_ref[0]
    tm = x.shape[0]
    h = _rms(x, gmix_ref[...]).astype(BF16)

    def proj(a, b):
        return _dot(h, w_ref[:, a:b])

    gates_ref[0] = _sigmoid(proj(_C_GATE, _C_CQ)).astype(BF16)

    cqn = _rms(proj(_C_CQ, _C_CKV), gq_ref[...]).astype(BF16)
    ckvn = _rms(proj(_C_CKV, _C_KRC), gkv_ref[...]).astype(BF16)

    ang = pos_ref[0].astype(F32) * invf_ref[...]
    cosf = jnp.cos(ang)
    sinf = jnp.sin(ang)
    lane = lax.broadcasted_iota(jnp.int32, (tm, LANES), 1)
    kr = proj(_C_KRC, _C_KRS) * cosf + proj(_C_KRS, _C_MU) * sinf
    q_scale = (MLA_NOPE + MLA_ROPE) ** -0.5 * math.log2(math.e)
    qmul = jnp.where(lane < MLA_NOPE, 1.0, jnp.where(lane < MLA_NOPE + MLA_ROPE, cosf, sinf)) * q_scale

    qe = _dot(cqn, wuq_ref[...])
    ke = _dot(ckvn, wuk_ref[...])
    ve = _dot(ckvn, wuv_ref[...]) + vones_ref[...]
    for hh in range(MLA_HEADS):
        sl = slice(hh * LANES, (hh + 1) * LANES)
        q_ref[0, hh] = (qe[:, sl] * qmul).astype(BF16)
        k_ref[0, hh] = (ke[:, sl] + kr).astype(BF16)
        v_ref[0, hh] = ve[:, sl].astype(BF16)

    mu_ref[0] = proj(_C_MU, _C_MV)
    mv_ref[0] = proj(_C_MV, _C_MO).astype(BF16)
    mo_ref[0] = _sigmoid(proj(_C_MO, _C_MEMQ)).astype(BF16)
    memq_ref[0] = proj(_C_MEMQ, _C_MGI).astype(BF16)
    mg_ref[0] = proj(_C_MGI, _C_END) + gbias_ref[...]


def _inproj(x, pos, g_mix, w_main, gbias, g_q, g_kv, wuq, wuk, wuv, vones, invf, tm):
    B, S, D = x.shape
    grid = (B, S // tm)
    tok = lambda w: pl.BlockSpec((1, tm, w), lambda b, i: (b, i, 0))
    head = pl.BlockSpec((1, MLA_HEADS, tm, LANES), lambda b, i: (b, 0, i, 0))
    out_shape = (
        jax.ShapeDtypeStruct((B, S, N_BRANCH * D_MODEL), BF16),
        jax.ShapeDtypeStruct((B, MLA_HEADS, S, LANES), BF16),
        jax.ShapeDtypeStruct((B, MLA_HEADS, S, LANES), BF16),
        jax.ShapeDtypeStruct((B, MLA_HEADS, S, LANES), BF16),
        jax.ShapeDtypeStruct((B, S, ML_WIDTH), F32),
        jax.ShapeDtypeStruct((B, S, ML_WIDTH), BF16),
        jax.ShapeDtypeStruct((B, S, ML_WIDTH), BF16),
        jax.ShapeDtypeStruct((B, S, 2 * LANES), F32),
        jax.ShapeDtypeStruct((B, S, MEM_WIDTH), BF16),
    )
    out_specs = (tok(N_BRANCH * D_MODEL), head, head, head, tok(ML_WIDTH), tok(ML_WIDTH),
                 tok(ML_WIDTH), tok(2 * LANES), tok(MEM_WIDTH))
    in_specs = [tok(D), pl.BlockSpec((1, tm, 1), lambda b, i: (b, i, 0))] + [
        _const_spec(a.shape) for a in (g_mix, w_main, gbias, g_q, g_kv, wuq, wuk, wuv, vones, invf)]
    return pl.pallas_call(
        _inproj_kernel, grid=grid, in_specs=in_specs, out_specs=out_specs, out_shape=out_shape,
        compiler_params=pltpu.CompilerParams(dimension_semantics=("parallel", "parallel"),
                                             vmem_limit_bytes=VMEM_LIMIT),
        name="inproj",
    )(x, pos, g_mix, w_main, gbias, g_q, g_kv, wuq, wuk, wuv, vones, invf)


def _mla_kernel(q_ref, k_ref, v_ref, o_ref):
    outs = []
    for j in range(2):
        s = _dot_nt(q_ref[0, j], k_ref[0, j])
        m = jnp.max(s, axis=-1, keepdims=True)
        p = jnp.exp2(s - m).astype(BF16)
        r = _dot(p, v_ref[0, j])
        den = r[:, MLA_V:MLA_V + 1] if j == 0 else r[:, 0:1]
        outs.append(r / den)
    lane = lax.broadcasted_iota(jnp.int32, outs[0].shape, 1)
    o_ref[0] = jnp.where(lane < MLA_V, outs[0], outs[1]).astype(BF16)


def _mla_attention(q, k, v, tq):
    B, H, S, _ = q.shape
    grid = (B, H // 2, S // tq)
    return pl.pallas_call(
        _mla_kernel, grid=grid,
        in_specs=[pl.BlockSpec((1, 2, tq, LANES), lambda b, p, i: (b, p, i, 0)),
                  pl.BlockSpec((1, 2, S, LANES), lambda b, p, i: (b, p, 0, 0)),
                  pl.BlockSpec((1, 2, S, LANES), lambda b, p, i: (b, p, 0, 0))],
        out_specs=pl.BlockSpec((1, tq, LANES), lambda b, p, i: (b, i, p)),
        out_shape=jax.ShapeDtypeStruct((B, S, H * MLA_V), BF16),
        compiler_params=pltpu.CompilerParams(dimension_semantics=("parallel", "parallel", "parallel"),
                                             vmem_limit_bytes=VMEM_LIMIT),
        name="mla_attn",
    )(q, k, v)


def _split3(x):
    hi = x.astype(BF16)
    r = x - hi.astype(F32)
    mid = r.astype(BF16)
    lo = (r - mid.astype(F32)).astype(BF16)
    return hi, mid, lo


def _cummax_rows(x, reverse):
    n = x.shape[0]
    row = lax.broadcasted_iota(jnp.int32, x.shape, 0)
    sh = 1
    while sh < n:
        if reverse:
            cand = jnp.where(row < n - sh, pltpu.roll(x, n - sh, 0), -jnp.inf)
        else:
            cand = jnp.where(row >= sh, pltpu.roll(x, sh, 0), -jnp.inf)
        x = jnp.maximum(x, cand)
        sh *= 2
    return x


def _mlstm_kernel(u_ref, v_ref, so_ref, g_ref, cw_ref, wq_ref, wk_ref, ghead_ref, o_ref,
                  upad, q_s, k_s, h_s, c_s, m_s):
    S = u_ref.shape[1]
    L = ML_CHUNK
    nc = S // L
    npair = ML_HEADS // 2

    zpad = jnp.zeros((CONV_PAD, ML_WIDTH), F32)
    upad[0:CONV_PAD, :] = zpad
    upad[CONV_PAD + S:CONV_PAD + S + CONV_PAD, :] = zpad
    upad[CONV_PAD:CONV_PAD + S, :] = u_ref[0]
    half = CONV_WIDTH // 2
    for r in range(nc):
        acc = jnp.zeros((L, ML_WIDTH), F32)
        for w in range(CONV_WIDTH):
            start = CONV_PAD - half + w + r * L
            acc = acc + upad[start:start + L, :] * cw_ref[w:w + 1, :]
        c = (acc * _sigmoid(acc)).astype(BF16)
        q_s[r * L:(r + 1) * L, :] = _dot(c, wq_ref[...]).astype(BF16)
        k_s[r * L:(r + 1) * L, :] = _dot(c, wk_ref[...]).astype(BF16)

    c_s[...] = jnp.zeros(c_s.shape, F32)
    m_s[...] = jnp.full(m_s.shape, -jnp.inf, F32)

    def allowed(shape, d):
        t = lax.broadcasted_iota(jnp.int32, shape, 0)
        s = lax.broadcasted_iota(jnp.int32, shape, 1)
        s = jnp.where(s >= L, s - L, s)
        return t >= s if d == 0 else t <= s

    def chunk_dir(ci, d):
        lane = lax.broadcasted_iota(jnp.int32, (L, LANES), 1)
        lane2 = lax.broadcasted_iota(jnp.int32, (2 * L, LANES), 1)
        row2 = lax.broadcasted_iota(jnp.int32, (2 * L, LANES), 0)
        srow = lax.broadcasted_iota(jnp.int32, (LANES, 2 * LANES), 0)
        ones_cols = jnp.where(lane2 == jnp.where(row2 < L, 0, 1), 1.0, 0.0).astype(BF16)
        tri_bf = jnp.where(allowed((L, L), d), 1.0, 0.0).astype(BF16)
        mask2 = allowed((L, 2 * L), d)

        base = pl.multiple_of(ci * L, L)
        rows = pl.ds(base, L)
        gi = g_ref[0, rows, 0:LANES]
        gf = g_ref[0, rows, LANES:2 * LANES]
        lf = jnp.minimum(gf, 0.0) - jnp.log1p(jnp.exp(-jnp.abs(gf)))
        hi, mid, lo = _split3(lf)
        bcum = _dot(tri_bf, hi) + _dot(tri_bf, mid) + _dot(tri_bf, lo)
        btot = jnp.sum(lf, axis=0, keepdims=True)
        mprev = m_s[d]
        cc = gi - bcum
        mstab = jnp.maximum(mprev, _cummax_rows(cc, reverse=(d == 1)))
        mt = bcum + mstab
        alpha = -mstab
        winter = jnp.exp(mprev - mstab)
        emt = jnp.exp(-mt)
        gend = btot + cc
        mnew = jnp.maximum(btot + mprev, jnp.max(gend, axis=0, keepdims=True))
        decay = jnp.exp(btot + mprev - mnew)
        ws = jnp.exp(gend - mnew)
        m_s[d] = mnew
        cct = cc.T

        for p in range(npair):
            le = 8 * d + 2 * p
            lo_ = le + 1
            sl = slice(p * LANES, (p + 1) * LANES)
            qp = q_s[rows, sl]
            kp = k_s[rows, sl]
            vp = v_ref[0, rows, sl]
            zero = jnp.zeros_like(kp)
            y = jnp.concatenate([jnp.where(lane < ML_DK, kp, zero), jnp.where(lane >= ML_DK, kp, zero)], axis=0)
            qk = _dot_nt(qp, y)
            expo = jnp.concatenate([alpha[:, le:le + 1] + cct[le:le + 1, :],
                                    alpha[:, lo_:lo_ + 1] + cct[lo_:lo_ + 1, :]], axis=1)
            wmat = jnp.exp(jnp.where(mask2, expo, -jnp.inf))
            sm = (qk * wmat).astype(BF16)
            v2 = jnp.concatenate(
                [jnp.concatenate([jnp.where(lane < ML_DV, vp, zero), jnp.where(lane >= ML_DV, vp, zero)], axis=0),
                 ones_cols], axis=1)
            qs = (qp.astype(F32) * jnp.where(lane < ML_DK, winter[:, le:le + 1], winter[:, lo_:lo_ + 1])).astype(BF16)
            cst = c_s[d, p]
            r = _dot(sm, v2) + _dot(qs, cst.astype(BF16))
            nrm = jnp.where(lane < ML_DV,
                            jnp.maximum(jnp.abs(r[:, LANES:LANES + 1]), emt[:, le:le + 1]),
                            jnp.maximum(jnp.abs(r[:, LANES + 1:LANES + 2]), emt[:, lo_:lo_ + 1]))
            h_s[d, rows, sl] = r[:, 0:LANES] / nrm
            wscol = jnp.concatenate([jnp.broadcast_to(ws[:, le:le + 1], (L, 2 * LANES)),
                                     jnp.broadcast_to(ws[:, lo_:lo_ + 1], (L, 2 * LANES))], axis=0)
            wv = (v2.astype(F32) * wscol).astype(BF16)
            upd = _dot_tn(y, wv)
            drow = jnp.where(srow < ML_DK, decay[:, le:le + 1], decay[:, lo_:lo_ + 1])
            c_s[d, p] = cst * drow + upd

    def step(j, carry):
        chunk_dir(j, 0)
        chunk_dir(nc - 1 - j, 1)
        return carry

    lax.fori_loop(0, nc, step, 0)

    lane = lax.broadcasted_iota(jnp.int32, (L, LANES), 1)
    for r in range(nc):
        rs = slice(r * L, (r + 1) * L)
        for p in range(npair):
            sl = slice(p * LANES, (p + 1) * LANES)
            hp = h_s[0, rs, sl] + h_s[1, rs, sl]
            sq = hp * hp
            s_e = jnp.sum(jnp.where(lane < ML_DV, sq, 0.0), axis=-1, keepdims=True)
            s_o = jnp.sum(jnp.where(lane >= ML_DV, sq, 0.0), axis=-1, keepdims=True)
            ms = jnp.where(lane < ML_DV, s_e, s_o) * (1.0 / ML_DV)
            out = hp * lax.rsqrt(ms + EPS) * ghead_ref[:, sl] * so_ref[0, rs, sl].astype(F32)
            o_ref[0, rs, sl] = out.astype(BF16)


def _mlstm(u, v, so, g, conv_w, wq_bd, wk_bd, g_head):
    B, S, W = u.shape
    seq = lambda w: pl.BlockSpec((1, S, w), lambda b: (b, 0, 0))
    return pl.pallas_call(
        _mlstm_kernel, grid=(B,),
        in_specs=[seq(W), seq(W), seq(W), seq(2 * LANES)] + [
            _const_spec(a.shape) for a in (conv_w, wq_bd, wk_bd, g_head)],
        out_specs=seq(W),
        out_shape=jax.ShapeDtypeStruct((B, S, W), BF16),
        scratch_shapes=[
            pltpu.VMEM((S + 2 * CONV_PAD, W), F32),
            pltpu.VMEM((S, W), BF16),
            pltpu.VMEM((S, W), BF16),
            pltpu.VMEM((2, S, W), F32),
            pltpu.VMEM((2, ML_HEADS // 2, LANES, 2 * LANES), F32),
            pltpu.VMEM((2, 1, LANES), F32),
        ],
        compiler_params=pltpu.CompilerParams(dimension_semantics=("parallel",),
                                             vmem_limit_bytes=VMEM_LIMIT),
        name="mlstm",
    )(u, v, so, g, conv_w, wq_bd, wk_bd, g_head)


def _memkv_kernel(mem_ref, g_ref, w_ref, k_ref, v_ref):
    mn = _rms(mem_ref[0], g_ref[...]).astype(BF16)
    kv = _dot(mn, w_ref[...])
    k_ref[0] = kv[:, :MEM_WIDTH].astype(BF16)
    v_ref[0] = kv[:, MEM_WIDTH:].astype(BF16)


def _memkv(mem, mem_g, w_kv):
    B, M, D = mem.shape
    blk = lambda w: pl.BlockSpec((1, M, w), lambda b: (b, 0, 0))
    return pl.pallas_call(
        _memkv_kernel, grid=(B,),
        in_specs=[blk(D), _const_spec(mem_g.shape), _const_spec(w_kv.shape)],
        out_specs=(blk(MEM_WIDTH), blk(MEM_WIDTH)),
        out_shape=(jax.ShapeDtypeStruct((B, M, MEM_WIDTH), BF16),) * 2,
        compiler_params=pltpu.CompilerParams(dimension_semantics=("parallel",)),
        name="memkv",
    )(mem, mem_g, w_kv)


def _memattn_kernel(q_ref, k_ref, v_ref, o_ref):
    scale = MEM_HEAD_DIM ** -0.5
    for hh in range(MEM_HEADS):
        sl = slice(hh * MEM_HEAD_DIM, (hh + 1) * MEM_HEAD_DIM)
        s = _dot_nt(q_ref[0, :, sl], k_ref[0, :, sl]) * scale
        e = jnp.exp(s - jnp.max(s, axis=-1, keepdims=True))
        den = jnp.sum(e, axis=-1, keepdims=True)
        o_ref[0, :, sl] = (_dot(e.astype(BF16), v_ref[0, :, sl]) / den).astype(BF16)


def _memattn(q, k, v, tq):
    B, S, W = q.shape
    M = k.shape[1]
    return pl.pallas_call(
        _memattn_kernel, grid=(B, S // tq),
        in_specs=[pl.BlockSpec((1, tq, W), lambda b, i: (b, i, 0)),
                  pl.BlockSpec((1, M, W), lambda b, i: (b, 0, 0)),
                  pl.BlockSpec((1, M, W), lambda b, i: (b, 0, 0))],
        out_specs=pl.BlockSpec((1, tq, W), lambda b, i: (b, i, 0)),
        out_shape=jax.ShapeDtypeStruct((B, S, W), BF16),
        compiler_params=pltpu.CompilerParams(dimension_semantics=("parallel", "parallel")),
        name="memattn",
    )(q, k, v)


def _out_kernel(x_ref, gates_ref, omla_ref, oml_ref, omem_ref, wb_ref, wout_ref, gffn_ref,
                wg_ref, wu_ref, wd_ref, gfin_ref, o_ref):
    merged = None
    for b, oref in enumerate((omla_ref, oml_ref, omem_ref)):
        y = _dot(oref[...], wb_ref[b]) * gates_ref[:, b * D_MODEL:(b + 1) * D_MODEL].astype(F32)
        merged = y if merged is None else merged + y
    x1 = x_ref[...] + _dot(merged.astype(BF16), wout_ref[...])
    h2 = _rms(x1, gffn_ref[...]).astype(BF16)
    hg = _dot(h2, wg_ref[...])
    a = (hg * _sigmoid(hg) * _dot(h2, wu_ref[...])).astype(BF16)
    x2 = x1 + _dot(a, wd_ref[...])
    o_ref[...] = _rms(x2, gfin_ref[...])


def _out_block(x, gates, o_mla, o_ml, o_mem, w_branch, w_out, g_ffn, w_g, w_u, w_d, g_final, tm):
    T, D = x.shape
    tok = lambda w: pl.BlockSpec((tm, w), lambda i: (i, 0))
    consts = (w_branch, w_out, g_ffn, w_g, w_u, w_d, g_final)
    return pl.pallas_call(
        _out_kernel, grid=(T // tm,),
        in_specs=[tok(D), tok(N_BRANCH * D), tok(o_mla.shape[1]), tok(o_ml.shape[1]), tok(o_mem.shape[1])]
        + [_const_spec(a.shape) for a in consts],
        out_specs=tok(D),
        out_shape=jax.ShapeDtypeStruct((T, D), x.dtype),
        compiler_params=pltpu.CompilerParams(dimension_semantics=("parallel",),
                                             vmem_limit_bytes=VMEM_LIMIT),
        name="out_block",
    )(x, gates, o_mla, o_ml, o_mem, *consts)


def _prep_in_weights(w_in, gate_bias):
    sizes = (N_BRANCH * D_MODEL, MLA_Q_RANK, MLA_KV_RANK, MLA_ROPE, ML_WIDTH, ML_WIDTH, ML_WIDTH,
             N_ML_GATES, MEM_WIDTH)
    offs = np.cumsum((0,) + sizes)
    w_gate, w_cq, w_ckv, w_kr, w_mu, w_mv, w_mo, w_mg, w_memq = [
        w_in[:, offs[i]:offs[i + 1]] for i in range(len(sizes))]
    half = MLA_ROPE // 2
    y1, y2 = w_kr[:, :half], w_kr[:, half:]
    z = lambda n: jnp.zeros((D_MODEL, n), w_in.dtype)
    w_krc = jnp.concatenate([z(MLA_NOPE), y1, y2, y1, y2], axis=1)
    w_krs = jnp.concatenate([z(MLA_NOPE), -y2, y1, -y2, y1], axis=1)
    H = ML_HEADS
    w_gi = jnp.concatenate([w_mg[:, 0:H], w_mg[:, 2 * H:3 * H], z(LANES - 2 * H)], axis=1)
    w_gf = jnp.concatenate([w_mg[:, H:2 * H], w_mg[:, 3 * H:4 * H], z(LANES - 2 * H)], axis=1)
    w_main = jnp.concatenate([w_gate, w_cq, w_ckv, w_krc, w_krs, w_mu, w_mv, w_mo, w_memq, w_gi, w_gf],
                             axis=1).astype(BF16)
    zb = jnp.zeros((LANES - 2 * H,), gate_bias.dtype)
    gbias = jnp.concatenate([gate_bias[0], gate_bias[2], zb, gate_bias[1], gate_bias[3], zb])[None, :]
    return w_main, gbias.astype(F32)


def _prep_mla_weights(w_uq, w_uk, w_uv):
    half = MLA_ROPE // 2
    qn = w_uq[:, :, :MLA_NOPE]
    x1 = w_uq[:, :, MLA_NOPE:MLA_NOPE + half]
    x2 = w_uq[:, :, MLA_NOPE + half:]
    wuq = jnp.concatenate([qn, x1, x2, -x2, x1], axis=-1).reshape(MLA_Q_RANK, MLA_HEADS * LANES)
    wuk = jnp.concatenate([w_uk, jnp.zeros_like(w_uk)], axis=-1).reshape(MLA_KV_RANK, MLA_HEADS * LANES)
    wv = w_uv.reshape(MLA_KV_RANK, MLA_HEADS // 2, 2, MLA_V)
    zv = jnp.zeros_like(wv[:, :, 0])
    wuv = jnp.concatenate([wv[:, :, 0], zv, zv, wv[:, :, 1]], axis=-1).reshape(MLA_KV_RANK, MLA_HEADS * LANES)
    ones_lane = np.zeros((1, MLA_HEADS * LANES), np.float32)
    for hh in range(MLA_HEADS):
        ones_lane[0, hh * LANES + (MLA_V if hh % 2 == 0 else 0)] = 1.0
    return wuq.astype(BF16), wuk.astype(BF16), wuv.astype(BF16), jnp.asarray(ones_lane)


def _block_diag(w):
    H, d, _ = w.shape
    eye = jnp.eye(H, dtype=w.dtype)
    return (eye[:, None, :, None] * w[:, :, None, :]).reshape(H * d, H * d)


def _layer(x, mem, positions, g_mix, w_in, mla_g_q, mla_g_kv, mla_w_uq, mla_w_uk, mla_w_uv,
           ml_conv_w, ml_w_q, ml_w_k, ml_gate_bias, ml_g_head, mem_g, mem_w_kv,
           w_branch, w_out, g_ffn, w_ffn_gate, w_ffn_up, w_ffn_down, g_final):
    B, S, D = x.shape
    row = lambda g: g.reshape(1, -1).astype(F32)
    w_main, gbias = _prep_in_weights(w_in, ml_gate_bias)
    wuq, wuk, wuv, vones = _prep_mla_weights(mla_w_uq, mla_w_uk, mla_w_uv)
    half = MLA_ROPE // 2
    inv = ROPE_THETA ** (-jnp.arange(half, dtype=F32) / half)
    invf = jnp.tile(inv, LANES // half)[None, :]

    gates, q, k, v, mu, mv, mo, mg, memq = _inproj(
        x, positions.reshape(B, S, 1), row(g_mix), w_main, gbias, row(mla_g_q), row(mla_g_kv),
        wuq, wuk, wuv, vones, invf, tm=min(256, S))
    o_mla = _mla_attention(q, k, v, tq=min(512, S))
    o_ml = _mlstm(mu, mv, mo, mg, ml_conv_w.astype(F32), _block_diag(ml_w_q).astype(BF16),
                  _block_diag(ml_w_k * (ML_DK ** -0.5)).astype(BF16), row(ml_g_head))
    mk, mvv = _memkv(mem, row(mem_g), mem_w_kv.astype(BF16))
    o_mem = _memattn(memq, mk, mvv, tq=min(512, S))

    T = B * S
    flat = lambda a: a.reshape(T, a.shape[-1])
    out = _out_block(flat(x), flat(gates), flat(o_mla), flat(o_ml), flat(o_mem),
                     w_branch.astype(BF16), w_out.astype(BF16), row(g_ffn), w_ffn_gate.astype(BF16),
                     w_ffn_up.astype(BF16), w_ffn_down.astype(BF16), row(g_final), tm=min(256, T))
    return out.reshape(B, S, D)


def kernel(x, mem, positions, g_mix, w_in, mla_g_q, mla_g_kv, mla_w_uq, mla_w_uk, mla_w_uv, ml_conv_w,
           ml_w_q, ml_w_k, ml_gate_bias, ml_g_head, mem_g, mem_w_kv, w_branch, w_out, g_ffn,
           w_ffn_gate, w_ffn_up, w_ffn_down, g_final):
    depth = g_mix.shape[0]
    assert depth == 1, "the final norm is fused into the single layer's output kernel"
    return _layer(x, mem, positions, g_mix[0], w_in[0], mla_g_q[0], mla_g_kv[0], mla_w_uq[0],
                  mla_w_uk[0], mla_w_uv[0], ml_conv_w[0], ml_w_q[0], ml_w_k[0], ml_gate_bias[0],
                  ml_g_head[0], mem_g[0], mem_w_kv[0], w_branch[0], w_out[0], g_ffn[0],
                  w_ffn_gate[0], w_ffn_up[0], w_ffn_down[0], g_final)
```

```python
import math

import numpy as np
import jax
import jax.numpy as jnp
from jax import lax
from jax.experimental import pallas as pl
from jax.experimental.pallas import tpu as pltpu

D_MODEL = 1024
MEM_LEN = 256
EPS = 1e-6
MLA_HEADS = 8
MLA_NOPE = 64
MLA_ROPE = 32
MLA_V = 64
MLA_Q_RANK = 384
MLA_KV_RANK = 256
ROPE_THETA = 10000.0
ML_HEADS = 8
ML_DK = 64
ML_DV = 64
ML_WIDTH = ML_HEADS * ML_DV
CONV_WIDTH = 5
MEM_HEADS = 4
MEM_HEAD_DIM = 128
MEM_WIDTH = MEM_HEADS * MEM_HEAD_DIM
N_BRANCH = 3
N_ML_GATES = 4 * ML_HEADS
D_FF = 2816

LANES = 128
ML_CHUNK = LANES
ML_PAIRS = ML_HEADS // 2
ML_REPL = 6
ML_DIR_LANES = ML_REPL * ML_HEADS
NEG_INIT = -1e30
CONV_PAD = 8
VMEM_LIMIT = 56 * 1024 * 1024

BF16 = jnp.bfloat16
F32 = jnp.float32

_C_GATE = 0
_C_CQ = _C_GATE + N_BRANCH * D_MODEL
_C_CKV = _C_CQ + MLA_Q_RANK
_C_KRC = _C_CKV + MLA_KV_RANK
_C_KRS = _C_KRC + LANES
_C_MU = _C_KRS + LANES
_C_MV = _C_MU + ML_WIDTH
_C_MO = _C_MV + ML_WIDTH
_C_MEMQ = _C_MO + ML_WIDTH
_C_MGI = _C_MEMQ + MEM_WIDTH
_C_MGF = _C_MGI + LANES
_C_END = _C_MGF + LANES


def _const_spec(shape):
    nd = len(shape)
    return pl.BlockSpec(shape, lambda *_: (0,) * nd, pipeline_mode=pl.Buffered(1))


def _sigmoid(x):
    return 1.0 / (1.0 + jnp.exp(-x))


def _rms(x, g):
    return x * lax.rsqrt(jnp.mean(x * x, axis=-1, keepdims=True) + EPS) * g


def _dot(a, b):
    return jnp.dot(a, b, preferred_element_type=F32)


def _dot_nt(a, b):
    return lax.dot_general(a, b, (((1,), (1,)), ((), ())), preferred_element_type=F32)


def _inproj_kernel(x_ref, pos_ref, gmix_ref, w_ref, gbias_ref, gq_ref, gkv_ref, wuq_ref, wuk_ref,
                   wuv_ref, vones_ref, invf_ref,
                   gates_ref, q_ref, k_ref, v_ref, mu_ref, mv_ref, mo_ref, mg_ref, memq_ref):
    x = x_ref[0]
    tm = x.shape[0]
    h = _rms(x, gmix_ref[...]).astype(BF16)

    def proj(a, b):
        return _dot(h, w_ref[:, a:b])

    gates_ref[0] = _sigmoid(proj(_C_GATE, _C_CQ)).astype(BF16)

    cqn = _rms(proj(_C_CQ, _C_CKV), gq_ref[...]).astype(BF16)
    ckvn = _rms(proj(_C_CKV, _C_KRC), gkv_ref[...]).astype(BF16)

    ang = pos_ref[0].astype(F32) * invf_ref[...]
    cosf = jnp.cos(ang)
    sinf = jnp.sin(ang)
    lane = lax.broadcasted_iota(jnp.int32, (tm, LANES), 1)
    kr = proj(_C_KRC, _C_KRS) * cosf + proj(_C_KRS, _C_MU) * sinf
    q_scale = (MLA_NOPE + MLA_ROPE) ** -0.5 * math.log2(math.e)
    qmul = jnp.where(lane < MLA_NOPE, 1.0, jnp.where(lane < MLA_NOPE + MLA_ROPE, cosf, sinf)) * q_scale

    qe = _dot(cqn, wuq_ref[...])
    ke = _dot(ckvn, wuk_ref[...])
    ve = _dot(ckvn, wuv_ref[...]) + vones_ref[...]
    for hh in range(MLA_HEADS):
        sl = slice(hh * LANES, (hh + 1) * LANES)
        q_ref[0, hh] = (qe[:, sl] * qmul).astype(BF16)
        k_ref[0, hh] = (ke[:, sl] + kr).astype(BF16)
        v_ref[0, hh] = ve[:, sl].astype(BF16)

    mu_ref[0] = proj(_C_MU, _C_MV)
    mv_ref[0] = proj(_C_MV, _C_MO).astype(BF16)
    mo_ref[0] = _sigmoid(proj(_C_MO, _C_MEMQ)).astype(BF16)
    memq_ref[0] = proj(_C_MEMQ, _C_MGI).astype(BF16)
    mg_ref[0] = proj(_C_MGI, _C_END) + gbias_ref[...]


def _inproj(x, pos, g_mix, w_main, gbias, g_q, g_kv, wuq, wuk, wuv, vones, invf, tm):
    B, S, D = x.shape
    grid = (B, S // tm)
    tok = lambda w: pl.BlockSpec((1, tm, w), lambda b, i: (b, i, 0))
    head = pl.BlockSpec((1, MLA_HEADS, tm, LANES), lambda b, i: (b, 0, i, 0))
    out_shape = (
        jax.ShapeDtypeStruct((B, S, N_BRANCH * D_MODEL), BF16),
        jax.ShapeDtypeStruct((B, MLA_HEADS, S, LANES), BF16),
        jax.ShapeDtypeStruct((B, MLA_HEADS, S, LANES), BF16),
        jax.ShapeDtypeStruct((B, MLA_HEADS, S, LANES), BF16),
        jax.ShapeDtypeStruct((B, S, ML_WIDTH), F32),
        jax.ShapeDtypeStruct((B, S, ML_WIDTH), BF16),
        jax.ShapeDtypeStruct((B, S, ML_WIDTH), BF16),
        jax.ShapeDtypeStruct((B, S, 2 * LANES), F32),
        jax.ShapeDtypeStruct((B, S, MEM_WIDTH), BF16),
    )
    out_specs = (tok(N_BRANCH * D_MODEL), head, head, head, tok(ML_WIDTH), tok(ML_WIDTH),
                 tok(ML_WIDTH), tok(2 * LANES), tok(MEM_WIDTH))
    in_specs = [tok(D), pl.BlockSpec((1, tm, 1), lambda b, i: (b, i, 0))] + [
        _const_spec(a.shape) for a in (g_mix, w_main, gbias, g_q, g_kv, wuq, wuk, wuv, vones, invf)]
    return pl.pallas_call(
        _inproj_kernel, grid=grid, in_specs=in_specs, out_specs=out_specs, out_shape=out_shape,
        compiler_params=pltpu.CompilerParams(dimension_semantics=("parallel", "parallel"),
                                             vmem_limit_bytes=VMEM_LIMIT),
        name="inproj",
    )(x, pos, g_mix, w_main, gbias, g_q, g_kv, wuq, wuk, wuv, vones, invf)


def _mla_kernel(q_ref, k_ref, v_ref, o_ref):
    outs = []
    for j in range(2):
        s = _dot_nt(q_ref[0, j], k_ref[0, j])
        m = jnp.max(s, axis=-1, keepdims=True)
        p = jnp.exp2(s - m).astype(BF16)
        r = _dot(p, v_ref[0, j])
        den = r[:, MLA_V:MLA_V + 1] if j == 0 else r[:, 0:1]
        outs.append(r / den)
    lane = lax.broadcasted_iota(jnp.int32, outs[0].shape, 1)
    o_ref[0] = jnp.where(lane < MLA_V, outs[0], outs[1]).astype(BF16)


def _mla_attention(q, k, v, tq):
    B, H, S, _ = q.shape
    grid = (B, H // 2, S // tq)
    return pl.pallas_call(
        _mla_kernel, grid=grid,
        in_specs=[pl.BlockSpec((1, 2, tq, LANES), lambda b, p, i: (b, p, i, 0)),
                  pl.BlockSpec((1, 2, S, LANES), lambda b, p, i: (b, p, 0, 0)),
                  pl.BlockSpec((1, 2, S, LANES), lambda b, p, i: (b, p, 0, 0))],
        out_specs=pl.BlockSpec((1, tq, LANES), lambda b, p, i: (b, i, p)),
        out_shape=jax.ShapeDtypeStruct((B, S, H * MLA_V), BF16),
        compiler_params=pltpu.CompilerParams(dimension_semantics=("parallel", "parallel", "parallel"),
                                             vmem_limit_bytes=VMEM_LIMIT),
        name="mla_attn",
    )(q, k, v)


def _gate_lane(d, r, h):
    return ML_DIR_LANES * d + ML_HEADS * r + h


def _split3(x):
    hi = x.astype(BF16).astype(F32)
    r = x - hi
    mid = r.astype(BF16).astype(F32)
    lo = (r - mid).astype(BF16).astype(F32)
    return hi, mid, lo


def _pieces(srcs, rep):
    out = None
    for i, src in enumerate(srcs):
        for j, piece in enumerate(_split3(src)):
            out = piece if out is None else jnp.where(rep == 3 * i + j, piece, out)
    return out.astype(BF16)


def _cummax_dirs(x, fwd_lane):
    n = x.shape[0]
    row = lax.broadcasted_iota(jnp.int32, x.shape, 0)
    sh = 1
    while sh < n:
        up = jnp.where(row >= sh, pltpu.roll(x, sh, 0), -jnp.inf)
        dn = jnp.where(row < n - sh, pltpu.roll(x, n - sh, 0), -jnp.inf)
        x = jnp.maximum(x, jnp.where(fwd_lane, up, dn))
        sh *= 2
    return x


def _mlstm_selectors():
    sel_a = np.zeros((LANES, 2, ML_PAIRS, 4 * LANES), np.float32)
    sel_m = np.zeros((LANES, 2, ML_PAIRS, LANES), np.float32)
    sel_d = np.zeros((LANES, 2, ML_PAIRS, 4 * LANES), np.float32)
    for d in range(2):
        for p in range(ML_PAIRS):
            for j in range(2):
                h = 2 * p + j
                for r in range(3):
                    a = _gate_lane(d, r, h)
                    g = _gate_lane(d, r + 3, h)
                    sel_a[a, d, p, j * LANES:(j + 1) * LANES] = 1
                    sel_a[a, d, p, 2 * LANES + j * ML_DK:2 * LANES + (j + 1) * ML_DK] = 1
                    sel_a[g, d, p, 3 * LANES + j * ML_DV:3 * LANES + (j + 1) * ML_DV] = 1
                    sel_m[a, d, p, j * ML_DK:(j + 1) * ML_DK] = 1
                    sel_d[a, d, p, j * 2 * LANES:(j + 1) * 2 * LANES] = 1
    as_bf = lambda m: jnp.asarray(m.reshape(LANES, -1), BF16)
    return as_bf(sel_a), as_bf(sel_m), as_bf(sel_d)


def _mlstm_kernel(u_ref, v_ref, so_ref, g_ref, cw_ref, wq_ref, wkt_ref, ghead_ref, sela_ref, selm_ref,
                  seld_ref, o_ref,
                  work, q_s, kt_s, gate_s, a_s, ct_s, wst_s, rows_s, mpb_s, dec_s, c_s):
    S = u_ref.shape[1]
    L = ML_CHUNK
    nc = S // L
    half = CONV_WIDTH // 2

    zpad = jnp.zeros((CONV_PAD, ML_WIDTH), F32)
    work[0:CONV_PAD, :] = zpad
    work[CONV_PAD + S:CONV_PAD + S + CONV_PAD, :] = zpad
    work[CONV_PAD:CONV_PAD + S, :] = u_ref[0]
    for r in range(nc):
        acc = jnp.zeros((L, ML_WIDTH), F32)
        for w in range(CONV_WIDTH):
            start = CONV_PAD - half + w + r * L
            acc = acc + work[start:start + L, :] * cw_ref[w:w + 1, :]
        c = (acc * _sigmoid(acc)).astype(BF16)
        q_s[r * L:(r + 1) * L, :] = _dot(c, wq_ref[...]).astype(BF16)
        kt_s[r] = _dot_nt(wkt_ref[...], c).astype(BF16)
    work[0:S, :] = jnp.zeros((S, ML_WIDTH), F32)
    c_s[...] = jnp.zeros(c_s.shape, F32)
    rows_s[...] = jnp.zeros(rows_s.shape, F32)

    def gate_lanes(nrows):
        lane = lax.broadcasted_iota(jnp.int32, (nrows, LANES), 1)
        fwd = lane < ML_DIR_LANES
        return fwd, jnp.right_shift(jnp.where(fwd, lane, lane - ML_DIR_LANES), 3)

    fwd_lane, rep = gate_lanes(L)
    fwd_row, _ = gate_lanes(1)
    tri_l = jnp.where(lax.broadcasted_iota(jnp.int32, (L, L), 0) >= lax.broadcasted_iota(jnp.int32, (L, L), 1),
                      1.0, 0.0).astype(BF16)
    btot, gmax = [], []
    for c in range(nc):
        rs = slice(c * L, (c + 1) * L)
        gf = g_ref[0, rs, LANES:2 * LANES]
        lf = jnp.minimum(gf, 0.0) - jnp.log1p(jnp.exp(-jnp.abs(gf)))
        bf = sum(_dot(tri_l, piece.astype(BF16)) for piece in _split3(lf))
        bt = bf[L - 1:L, :]
        bcum = jnp.where(fwd_lane, bf, bt - bf + lf)
        cc = g_ref[0, rs, 0:LANES] - bcum
        cmx = _cummax_dirs(cc, fwd_lane)
        gate_s[0, rs, :] = bcum
        gate_s[1, rs, :] = cc
        gate_s[2, rs, :] = cmx
        btot.append(bt)
        gmax.append(jnp.where(fwd_row, cmx[L - 1:L, :], cmx[0:1, :]))

    def scan_m(order):
        prev, new = [None] * nc, [None] * nc
        m = jnp.full((1, LANES), NEG_INIT, F32)
        for c in order:
            prev[c] = m
            m = jnp.maximum(btot[c] + m, btot[c] + gmax[c])
            new[c] = m
        return prev, new

    prev_f, new_f = scan_m(range(nc))
    prev_b, new_b = scan_m(reversed(range(nc)))

    for c in range(nc):
        rs = slice(c * L, (c + 1) * L)
        mprev = jnp.where(fwd_row, prev_f[c], prev_b[c])
        mnew = jnp.where(fwd_row, new_f[c], new_b[c])
        bcum, cc, cmx = gate_s[0, rs, :], gate_s[1, rs, :], gate_s[2, rs, :]
        mstab = jnp.maximum(mprev, cmx)
        a_s[rs, :] = _pieces((-mstab, -(bcum + mstab)), rep)
        ct_s[c] = cc.T
        wst_s[c] = jnp.exp(btot[c] + cc - mnew).T
        rows_s[0, c:c + 1, :] = mprev
        rows_s[1, c:c + 1, :] = jnp.exp(btot[c] + mprev - mnew)
    _, rep_c = gate_lanes(rows_s.shape[1])
    mpb = _dot(_pieces((rows_s[0],), rep_c), selm_ref[...])
    dec = _dot(_pieces((rows_s[1],), rep_c), seld_ref[...])
    for c in range(nc):
        mpb_s[c] = mpb[c:c + 1, :]
        dec_s[c] = dec[c:c + 1, :]

    def chunk_dir(ci, d):
        rows = pl.ds(pl.multiple_of(ci * L, L), L)
        t_i = lax.broadcasted_iota(jnp.int32, (L, L), 0)
        s_i = lax.broadcasted_iota(jnp.int32, (L, L), 1)
        allowed = t_i >= s_i if d == 0 else t_i <= s_i
        lane_ = lax.broadcasted_iota(jnp.int32, (L, LANES), 1)
        low = lane_ < ML_DV
        ones_e = jnp.where(low, 1.0, 0.0).astype(BF16)
        ones_o = jnp.where(low, 0.0, 1.0).astype(BF16)
        blk = 4 * LANES
        z = _dot(a_s[rows, :], sela_ref[:, d * ML_PAIRS * blk:(d + 1) * ML_PAIRS * blk])
        for p in range(ML_PAIRS):
            le = _gate_lane(d, 0, 2 * p)
            lo = le + 1
            dp = d * ML_PAIRS + p
            sl = slice(p * LANES, (p + 1) * LANES)
            zb = z[:, p * blk:(p + 1) * blk]
            qp = q_s[rows, sl]
            ktp = kt_s[ci, sl, :]
            vp = v_ref[0, rows, sl]
            zk = jnp.zeros((ML_DK, L), BF16)
            yt = jnp.concatenate([jnp.concatenate([ktp[:ML_DK], zk], axis=0),
                                  jnp.concatenate([zk, ktp[ML_DK:]], axis=0)], axis=1)
            qk = _dot(qp, yt)
            e_e = jnp.where(allowed, zb[:, 0:LANES] + ct_s[ci, le:le + 1, :], -jnp.inf)
            e_o = jnp.where(allowed, zb[:, LANES:2 * LANES] + ct_s[ci, lo:lo + 1, :], -jnp.inf)
            e_q = zb[:, 2 * LANES:3 * LANES] + mpb_s[ci, :, dp * LANES:(dp + 1) * LANES]
            pw = jnp.exp(jnp.concatenate([e_e, e_o, e_q], axis=1))
            x = (pw * jnp.concatenate([qk, qp.astype(F32)], axis=1)).astype(BF16)
            zero = jnp.zeros_like(vp)
            v2 = jnp.concatenate([jnp.concatenate([jnp.where(low, vp, zero), ones_e], axis=1),
                                  jnp.concatenate([jnp.where(low, zero, vp), ones_o], axis=1)], axis=0)
            cst = c_s[d, p]
            r = _dot(x, jnp.concatenate([v2, cst.astype(BF16)], axis=0))
            clamp = jnp.exp(zb[:, 3 * LANES:4 * LANES])
            work[rows, sl] = work[rows, sl] + r[:, 0:LANES] / jnp.maximum(jnp.abs(r[:, LANES:2 * LANES]), clamp)
            wsrow = jnp.concatenate([wst_s[ci, le:le + 1, :], wst_s[ci, lo:lo + 1, :]], axis=1)
            upd = _dot((yt.astype(F32) * wsrow).astype(BF16), v2)
            drow = dec_s[ci, :, dp * blk:(dp + 1) * blk]
            dmat = jnp.concatenate([jnp.broadcast_to(drow[:, 0:2 * LANES], (ML_DK, 2 * LANES)),
                                    jnp.broadcast_to(drow[:, 2 * LANES:4 * LANES], (ML_DK, 2 * LANES))], axis=0)
            c_s[d, p] = cst * dmat + upd

    def step(j, carry):
        chunk_dir(j, 0)
        chunk_dir(nc - 1 - j, 1)
        return carry

    lax.fori_loop(0, nc, step, 0)

    low = lax.broadcasted_iota(jnp.int32, (L, LANES), 1) < ML_DV
    for r in range(nc):
        rs = slice(r * L, (r + 1) * L)
        for p in range(ML_PAIRS):
            sl = slice(p * LANES, (p + 1) * LANES)
            hp = work[rs, sl]
            sq = hp * hp
            s_e = jnp.sum(jnp.where(low, sq, 0.0), axis=-1, keepdims=True)
            s_o = jnp.sum(jnp.where(low, 0.0, sq), axis=-1, keepdims=True)
            ms = jnp.where(low, s_e, s_o) * (1.0 / ML_DV)
            out = hp * lax.rsqrt(ms + EPS) * ghead_ref[:, sl] * so_ref[0, rs, sl].astype(F32)
            o_ref[0, rs, sl] = out.astype(BF16)


def _mlstm(u, v, so, g, conv_w, wq_bd, wkt_bd, g_head):
    B, S, W = u.shape
    L = ML_CHUNK
    nc = S // L
    sels = _mlstm_selectors()
    seq = lambda w: pl.BlockSpec((1, S, w), lambda b: (b, 0, 0))
    consts = (conv_w, wq_bd, wkt_bd, g_head) + sels
    return pl.pallas_call(
        _mlstm_kernel, grid=(B,),
        in_specs=[seq(W), seq(W), seq(W), seq(2 * LANES)] + [_const_spec(a.shape) for a in consts],
        out_specs=seq(W),
        out_shape=jax.ShapeDtypeStruct((B, S, W), BF16),
        scratch_shapes=[
            pltpu.VMEM((S + 2 * CONV_PAD, W), F32),
            pltpu.VMEM((S, W), BF16),
            pltpu.VMEM((nc, W, L), BF16),
            pltpu.VMEM((3, S, LANES), F32),
            pltpu.VMEM((S, LANES), BF16),
            pltpu.VMEM((nc, LANES, L), F32),
            pltpu.VMEM((nc, LANES, L), F32),
            pltpu.VMEM((2, -(-nc // 16) * 16, LANES), F32),
            pltpu.VMEM((nc, 1, 2 * ML_PAIRS * LANES), F32),
            pltpu.VMEM((nc, 1, 2 * ML_PAIRS * 4 * LANES), F32),
            pltpu.VMEM((2, ML_PAIRS, LANES, 2 * LANES), F32),
        ],
        compiler_params=pltpu.CompilerParams(dimension_semantics=("parallel",),
                                             vmem_limit_bytes=VMEM_LIMIT),
        name="mlstm",
    )(u, v, so, g, *consts)


def _memkv_kernel(mem_ref, g_ref, w_ref, k_ref, v_ref):
    mn = _rms(mem_ref[0], g_ref[...]).astype(BF16)
    kv = _dot(mn, w_ref[...])
    k_ref[0] = kv[:, :MEM_WIDTH].astype(BF16)
    v_ref[0] = kv[:, MEM_WIDTH:].astype(BF16)


def _memkv(mem, mem_g, w_kv):
    B, M, D = mem.shape
    blk = lambda w: pl.BlockSpec((1, M, w), lambda b: (b, 0, 0))
    return pl.pallas_call(
        _memkv_kernel, grid=(B,),
        in_specs=[blk(D), _const_spec(mem_g.shape), _const_spec(w_kv.shape)],
        out_specs=(blk(MEM_WIDTH), blk(MEM_WIDTH)),
        out_shape=(jax.ShapeDtypeStruct((B, M, MEM_WIDTH), BF16),) * 2,
        compiler_params=pltpu.CompilerParams(dimension_semantics=("parallel",)),
        name="memkv",
    )(mem, mem_g, w_kv)


def _memattn_kernel(q_ref, k_ref, v_ref, o_ref):
    scale = MEM_HEAD_DIM ** -0.5
    for hh in range(MEM_HEADS):
        sl = slice(hh * MEM_HEAD_DIM, (hh + 1) * MEM_HEAD_DIM)
        s = _dot_nt(q_ref[0, :, sl], k_ref[0, :, sl]) * scale
        e = jnp.exp(s - jnp.max(s, axis=-1, keepdims=True))
        den = jnp.sum(e, axis=-1, keepdims=True)
        o_ref[0, :, sl] = (_dot(e.astype(BF16), v_ref[0, :, sl]) / den).astype(BF16)


def _memattn(q, k, v, tq):
    B, S, W = q.shape
    M = k.shape[1]
    return pl.pallas_call(
        _memattn_kernel, grid=(B, S // tq),
        in_specs=[pl.BlockSpec((1, tq, W), lambda b, i: (b, i, 0)),
                  pl.BlockSpec((1, M, W), lambda b, i: (b, 0, 0)),
                  pl.BlockSpec((1, M, W), lambda b, i: (b, 0, 0))],
        out_specs=pl.BlockSpec((1, tq, W), lambda b, i: (b, i, 0)),
        out_shape=jax.ShapeDtypeStruct((B, S, W), BF16),
        compiler_params=pltpu.CompilerParams(dimension_semantics=("parallel", "parallel")),
        name="memattn",
    )(q, k, v)


def _out_kernel(x_ref, gates_ref, omla_ref, oml_ref, omem_ref, wb_ref, wout_ref, gffn_ref,
                wg_ref, wu_ref, wd_ref, gfin_ref, o_ref):
    merged = None
    for b, oref in enumerate((omla_ref, oml_ref, omem_ref)):
        y = _dot(oref[...], wb_ref[b]) * gates_ref[:, b * D_MODEL:(b + 1) * D_MODEL].astype(F32)
        merged = y if merged is None else merged + y
    x1 = x_ref[...] + _dot(merged.astype(BF16), wout_ref[...])
    h2 = _rms(x1, gffn_ref[...]).astype(BF16)
    hg = _dot(h2, wg_ref[...])
    a = (hg * _sigmoid(hg) * _dot(h2, wu_ref[...])).astype(BF16)
    x2 = x1 + _dot(a, wd_ref[...])
    o_ref[...] = _rms(x2, gfin_ref[...])


def _out_block(x, gates, o_mla, o_ml, o_mem, w_branch, w_out, g_ffn, w_g, w_u, w_d, g_final, tm):
    T, D = x.shape
    tok = lambda w: pl.BlockSpec((tm, w), lambda i: (i, 0))
    consts = (w_branch, w_out, g_ffn, w_g, w_u, w_d, g_final)
    return pl.pallas_call(
        _out_kernel, grid=(T // tm,),
        in_specs=[tok(D), tok(N_BRANCH * D), tok(o_mla.shape[1]), tok(o_ml.shape[1]), tok(o_mem.shape[1])]
        + [_const_spec(a.shape) for a in consts],
        out_specs=tok(D),
        out_shape=jax.ShapeDtypeStruct((T, D), x.dtype),
        compiler_params=pltpu.CompilerParams(dimension_semantics=("parallel",),
                                             vmem_limit_bytes=VMEM_LIMIT),
        name="out_block",
    )(x, gates, o_mla, o_ml, o_mem, *consts)


def _prep_in_weights(w_in, gate_bias):
    sizes = (N_BRANCH * D_MODEL, MLA_Q_RANK, MLA_KV_RANK, MLA_ROPE, ML_WIDTH, ML_WIDTH, ML_WIDTH,
             N_ML_GATES, MEM_WIDTH)
    offs = np.cumsum((0,) + sizes)
    w_gate, w_cq, w_ckv, w_kr, w_mu, w_mv, w_mo, w_mg, w_memq = [
        w_in[:, offs[i]:offs[i + 1]] for i in range(len(sizes))]
    half = MLA_ROPE // 2
    y1, y2 = w_kr[:, :half], w_kr[:, half:]
    z = lambda n: jnp.zeros((D_MODEL, n), w_in.dtype)
    w_krc = jnp.concatenate([z(MLA_NOPE), y1, y2, y1, y2], axis=1)
    w_krs = jnp.concatenate([z(MLA_NOPE), -y2, y1, -y2, y1], axis=1)
    H = ML_HEADS
    pad = LANES - 2 * ML_DIR_LANES

    def gate_block(cols, fwd, bwd, zeros):
        return jnp.concatenate([cols(fwd)] * ML_REPL + [cols(bwd)] * ML_REPL + [zeros], axis=-1)

    wcols = lambda g: w_mg[:, g * H:(g + 1) * H]
    w_gi = gate_block(wcols, 0, 2, z(pad))
    w_gf = gate_block(wcols, 1, 3, z(pad))
    w_main = jnp.concatenate([w_gate, w_cq, w_ckv, w_krc, w_krs, w_mu, w_mv, w_mo, w_memq, w_gi, w_gf],
                             axis=1).astype(BF16)
    bcols = lambda g: gate_bias[g]
    zb = jnp.zeros((pad,), gate_bias.dtype)
    gbias = jnp.concatenate([gate_block(bcols, 0, 2, zb), gate_block(bcols, 1, 3, zb)])[None, :]
    return w_main, gbias.astype(F32)


def _prep_mla_weights(w_uq, w_uk, w_uv):
    half = MLA_ROPE // 2
    qn = w_uq[:, :, :MLA_NOPE]
    x1 = w_uq[:, :, MLA_NOPE:MLA_NOPE + half]
    x2 = w_uq[:, :, MLA_NOPE + half:]
    wuq = jnp.concatenate([qn, x1, x2, -x2, x1], axis=-1).reshape(MLA_Q_RANK, MLA_HEADS * LANES)
    wuk = jnp.concatenate([w_uk, jnp.zeros_like(w_uk)], axis=-1).reshape(MLA_KV_RANK, MLA_HEADS * LANES)
    wv = w_uv.reshape(MLA_KV_RANK, MLA_HEADS // 2, 2, MLA_V)
    zv = jnp.zeros_like(wv[:, :, 0])
    wuv = jnp.concatenate([wv[:, :, 0], zv, zv, wv[:, :, 1]], axis=-1).reshape(MLA_KV_RANK, MLA_HEADS * LANES)
    ones_lane = np.zeros((1, MLA_HEADS * LANES), np.float32)
    for hh in range(MLA_HEADS):
        ones_lane[0, hh * LANES + (MLA_V if hh % 2 == 0 else 0)] = 1.0
    return wuq.astype(BF16), wuk.astype(BF16), wuv.astype(BF16), jnp.asarray(ones_lane)


def _block_diag(w):
    H, d, _ = w.shape
    eye = jnp.eye(H, dtype=w.dtype)
    return (eye[:, None, :, None] * w[:, :, None, :]).reshape(H * d, H * d)


def _layer(x, mem, positions, g_mix, w_in, mla_g_q, mla_g_kv, mla_w_uq, mla_w_uk, mla_w_uv,
           ml_conv_w, ml_w_q, ml_w_k, ml_gate_bias, ml_g_head, mem_g, mem_w_kv,
           w_branch, w_out, g_ffn, w_ffn_gate, w_ffn_up, w_ffn_down, g_final):
    B, S, D = x.shape
    row = lambda g: g.reshape(1, -1).astype(F32)
    w_main, gbias = _prep_in_weights(w_in, ml_gate_bias)
    wuq, wuk, wuv, vones = _prep_mla_weights(mla_w_uq, mla_w_uk, mla_w_uv)
    half = MLA_ROPE // 2
    inv = ROPE_THETA ** (-jnp.arange(half, dtype=F32) / half)
    invf = jnp.tile(inv, LANES // half)[None, :]

    gates, q, k, v, mu, mv, mo, mg, memq = _inproj(
        x, positions.reshape(B, S, 1), row(g_mix), w_main, gbias, row(mla_g_q), row(mla_g_kv),
        wuq, wuk, wuv, vones, invf, tm=min(256, S))
    o_mla = _mla_attention(q, k, v, tq=min(512, S))
    o_ml = _mlstm(mu, mv, mo, mg, ml_conv_w.astype(F32), _block_diag(ml_w_q).astype(BF16),
                  _block_diag(ml_w_k * (ML_DK ** -0.5)).T.astype(BF16), row(ml_g_head))
    mk, mvv = _memkv(mem, row(mem_g), mem_w_kv.astype(BF16))
    o_mem = _memattn(memq, mk, mvv, tq=min(512, S))

    T = B * S
    flat = lambda a: a.reshape(T, a.shape[-1])
    out = _out_block(flat(x), flat(gates), flat(o_mla), flat(o_ml), flat(o_mem),
                     w_branch.astype(BF16), w_out.astype(BF16), row(g_ffn), w_ffn_gate.astype(BF16),
                     w_ffn_up.astype(BF16), w_ffn_down.astype(BF16), row(g_final), tm=min(256, T))
    return out.reshape(B, S, D)


def kernel(x, mem, positions, g_mix, w_in, mla_g_q, mla_g_kv, mla_w_uq, mla_w_uk, mla_w_uv, ml_conv_w,
           ml_w_q, ml_w_k, ml_gate_bias, ml_g_head, mem_g, mem_w_kv, w_branch, w_out, g_ffn,
           w_ffn_gate, w_ffn_up, w_ffn_down, g_final):
    depth = g_mix.shape[0]
    assert depth == 1, "the final norm is fused into the single layer's output kernel"
    return _layer(x, mem, positions, g_mix[0], w_in[0], mla_g_q[0], mla_g_kv[0], mla_w_uq[0],
                  mla_w_uk[0], mla_w_uv[0], ml_conv_w[0], ml_w_q[0], ml_w_k[0], ml_gate_bias[0],
                  ml_g_head[0], mem_g[0], mem_w_kv[0], w_branch[0], w_out[0], g_ffn[0],
                  w_ffn_gate[0], w_ffn_up[0], w_ffn_down[0], g_final)
```

```python
import math

import numpy as np
import jax
import jax.numpy as jnp
from jax import lax
from jax.experimental import pallas as pl
from jax.experimental.pallas import tpu as pltpu

D_MODEL = 1024
MEM_LEN = 256
EPS = 1e-6
MLA_HEADS = 8
MLA_NOPE = 64
MLA_ROPE = 32
MLA_V = 64
MLA_Q_RANK = 384
MLA_KV_RANK = 256
ROPE_THETA = 10000.0
ML_HEADS = 8
ML_DK = 64
ML_DV = 64
ML_WIDTH = ML_HEADS * ML_DV
CONV_WIDTH = 5
MEM_HEADS = 4
MEM_HEAD_DIM = 128
MEM_WIDTH = MEM_HEADS * MEM_HEAD_DIM
N_BRANCH = 3
N_ML_GATES = 4 * ML_HEADS
D_FF = 2816

LANES = 128
MLA_UNIT_ROWS = 512
MLA_STEP_HEADS = 4
ML_CHUNK = LANES
ML_PAIRS = ML_HEADS // 2
ML_REPL = 6
ML_DIR_LANES = ML_REPL * ML_HEADS
NEG_INIT = -1e30
CONV_PAD = 8
VMEM_LIMIT = 56 * 1024 * 1024

BF16 = jnp.bfloat16
F32 = jnp.float32

_C_GATE = 0
_C_CQ = _C_GATE + N_BRANCH * D_MODEL
_C_CKV = _C_CQ + MLA_Q_RANK
_C_KR = _C_CKV + MLA_KV_RANK
_C_MU = _C_KR + LANES
_C_MV = _C_MU + ML_WIDTH
_C_MO = _C_MV + ML_WIDTH
_C_MEMQ = _C_MO + ML_WIDTH
_C_MGI = _C_MEMQ + MEM_WIDTH
_C_MGF = _C_MGI + LANES
_C_END = _C_MGF + LANES


def _const_spec(shape):
    nd = len(shape)
    return pl.BlockSpec(shape, lambda *_: (0,) * nd, pipeline_mode=pl.Buffered(1))


def _sigmoid(x):
    return 1.0 / (1.0 + jnp.exp(-x))


def _rms(x, g):
    return x * lax.rsqrt(jnp.mean(x * x, axis=-1, keepdims=True) + EPS) * g


def _dot(a, b):
    return jnp.dot(a, b, preferred_element_type=F32)


def _dot_nt(a, b):
    return lax.dot_general(a, b, (((1,), (1,)), ((), ())), preferred_element_type=F32)


def _inproj_kernel(x_ref, pos_ref, gmix_ref, w_ref, gbias_ref, gq_ref, gkv_ref, wuq_ref, wuk_ref,
                   wuv_ref, vones_ref, rope_ref,
                   gates_ref, q_ref, k_ref, v_ref, mu_ref, mv_ref, mo_ref, mg_ref, memq_ref):
    x = x_ref[0]
    tm = x.shape[0]
    lane = lax.broadcasted_iota(jnp.int32, (tm, LANES), 1)
    hm = tm // 2
    pos = pos_ref[0].astype(F32)
    pos2 = jnp.where(lax.broadcasted_iota(jnp.int32, (hm, LANES), 1) < MLA_NOPE, pos[:hm], pos[hm:])
    rope2 = jnp.cos(pos2 * rope_ref[0:1, :] - rope_ref[1:2, :])
    rope = jnp.concatenate([pltpu.roll(rope2, MLA_NOPE, 1), rope2], axis=0)
    h = _rms(x, gmix_ref[...]).astype(BF16)

    def proj(a, b):
        return _dot(h, w_ref[:, a:b])

    cqn = _rms(proj(_C_CQ, _C_CKV), gq_ref[...]).astype(BF16)
    ckvn = _rms(proj(_C_CKV, _C_KR), gkv_ref[...]).astype(BF16)

    kra = proj(_C_KR, _C_MU) * rope
    kr = jnp.where(lane < MLA_NOPE, 0.0,
                   kra + pltpu.roll(kra, MLA_ROPE, 1) + pltpu.roll(kra, LANES - MLA_ROPE, 1))
    q_scale = (MLA_NOPE + MLA_ROPE) ** -0.5 * math.log2(math.e)
    qmul = jnp.where(lane < MLA_NOPE, 1.0, rope) * q_scale

    qe = _dot(cqn, wuq_ref[...])
    ket = _dot_nt(wuk_ref[...], ckvn)
    krt = kr.T
    ve = _dot(ckvn, wuv_ref[...]) + vones_ref[...]
    for hh in range(MLA_HEADS):
        sl = slice(hh * LANES, (hh + 1) * LANES)
        q_ref[0, hh] = (qe[:, sl] * qmul).astype(BF16)
        k_ref[0, hh] = (ket[sl, :] + krt).astype(BF16)
        v_ref[0, hh] = ve[:, sl].astype(BF16)

    mu_ref[0] = proj(_C_MU, _C_MV)
    mv_ref[0] = proj(_C_MV, _C_MO).astype(BF16)
    mo_ref[0] = _sigmoid(proj(_C_MO, _C_MEMQ)).astype(BF16)
    memq_ref[0] = proj(_C_MEMQ, _C_MGI).astype(BF16)
    mg_ref[0] = proj(_C_MGI, _C_END) + gbias_ref[...]
    gates_ref[0] = _sigmoid(proj(_C_GATE, _C_CQ)).astype(BF16)


def _inproj(x, pos, g_mix, w_main, gbias, g_q, g_kv, wuq, wuk, wuv, vones, invf, tm):
    B, S, D = x.shape
    grid = (B, S // tm)
    tok = lambda w: pl.BlockSpec((1, tm, w), lambda b, i: (b, i, 0))
    head = pl.BlockSpec((1, MLA_HEADS, tm, LANES), lambda b, i: (b, 0, i, 0))
    head_t = pl.BlockSpec((1, MLA_HEADS, LANES, tm), lambda b, i: (b, 0, 0, i))
    out_shape = (
        jax.ShapeDtypeStruct((B, S, N_BRANCH * D_MODEL), BF16),
        jax.ShapeDtypeStruct((B, MLA_HEADS, S, LANES), BF16),
        jax.ShapeDtypeStruct((B, MLA_HEADS, LANES, S), BF16),
        jax.ShapeDtypeStruct((B, MLA_HEADS, S, LANES), BF16),
        jax.ShapeDtypeStruct((B, S, ML_WIDTH), F32),
        jax.ShapeDtypeStruct((B, S, ML_WIDTH), BF16),
        jax.ShapeDtypeStruct((B, S, ML_WIDTH), BF16),
        jax.ShapeDtypeStruct((B, S, 2 * LANES), F32),
        jax.ShapeDtypeStruct((B, S, MEM_WIDTH), BF16),
    )
    out_specs = (tok(N_BRANCH * D_MODEL), head, head_t, head, tok(ML_WIDTH), tok(ML_WIDTH),
                 tok(ML_WIDTH), tok(2 * LANES), tok(MEM_WIDTH))
    in_specs = [tok(D), pl.BlockSpec((1, tm, 1), lambda b, i: (b, i, 0))] + [
        _const_spec(a.shape) for a in (g_mix, w_main, gbias, g_q, g_kv, wuq, wuk, wuv, vones, invf)]
    return pl.pallas_call(
        _inproj_kernel, grid=grid, in_specs=in_specs, out_specs=out_specs, out_shape=out_shape,
        compiler_params=pltpu.CompilerParams(dimension_semantics=("parallel", "parallel"),
                                             vmem_limit_bytes=VMEM_LIMIT),
        name="inproj",
    )(x, pos, g_mix, w_main, gbias, g_q, g_kv, wuq, wuk, wuv, vones, invf)


def _mla_kernel(q_ref, kt_ref, v_ref, o_ref, s_buf, p_buf):
    S = q_ref.shape[2]
    tu = s_buf.shape[1]
    low = lax.broadcasted_iota(jnp.int32, (tu, LANES), 1) < MLA_V
    unit = 0
    for pair in range(MLA_STEP_HEADS // 2):
        for t in range(S // tu):
            rows = slice(t * tu, (t + 1) * tu)
            outs = []
            for j in range(2):
                hh = 2 * pair + j
                slot = unit % 2
                unit += 1
                s_buf[slot] = _dot(q_ref[0, hh, rows, :], kt_ref[0, hh])
                s = s_buf[slot]
                p_buf[slot] = jnp.exp2(s - jnp.max(s, axis=-1, keepdims=True)).astype(BF16)
                r = _dot(p_buf[slot], v_ref[0, hh])
                den = r[:, MLA_V:MLA_V + 1] if j == 0 else r[:, 0:1]
                outs.append(r / den)
            o_ref[0, rows, pair * LANES:(pair + 1) * LANES] = jnp.where(low, outs[0], outs[1]).astype(BF16)


def _mla_attention(q, kt, v):
    B, H, S, _ = q.shape
    hs = MLA_STEP_HEADS
    tu = min(MLA_UNIT_ROWS, S)
    heads = lambda shape: pl.BlockSpec((1, hs) + shape, lambda b, p: (b, p, 0, 0))
    return pl.pallas_call(
        _mla_kernel, grid=(B, H // hs),
        in_specs=[heads((S, LANES)), heads((LANES, S)), heads((S, LANES))],
        out_specs=pl.BlockSpec((1, S, hs * MLA_V), lambda b, p: (b, 0, p)),
        out_shape=jax.ShapeDtypeStruct((B, S, H * MLA_V), BF16),
        scratch_shapes=[pltpu.VMEM((2, tu, S), F32),
                        pltpu.VMEM((2, tu, S), BF16)],
        compiler_params=pltpu.CompilerParams(dimension_semantics=("parallel", "parallel"),
                                             vmem_limit_bytes=VMEM_LIMIT),
        name="mla_attn",
    )(q, kt, v)


def _gate_lane(d, r, h):
    return ML_DIR_LANES * d + ML_HEADS * r + h


def _split3(x):
    hi = x.astype(BF16).astype(F32)
    r = x - hi
    mid = r.astype(BF16).astype(F32)
    lo = (r - mid).astype(BF16).astype(F32)
    return hi, mid, lo


def _pieces(srcs, rep):
    out = None
    for i, src in enumerate(srcs):
        for j, piece in enumerate(_split3(src)):
            out = piece if out is None else jnp.where(rep == 3 * i + j, piece, out)
    return out.astype(BF16)


def _cummax_dirs(x, fwd_lane):
    n = x.shape[0]
    row = lax.broadcasted_iota(jnp.int32, x.shape, 0)
    sh = 1
    while sh < n:
        up = jnp.where(row >= sh, pltpu.roll(x, sh, 0), -jnp.inf)
        dn = jnp.where(row < n - sh, pltpu.roll(x, n - sh, 0), -jnp.inf)
        x = jnp.maximum(x, jnp.where(fwd_lane, up, dn))
        sh *= 2
    return x


def _mlstm_selectors():
    sel_a = np.zeros((LANES, 2, ML_PAIRS, 4 * LANES), np.float32)
    sel_m = np.zeros((LANES, 2, ML_PAIRS, LANES), np.float32)
    sel_d = np.zeros((LANES, 2, ML_PAIRS, 4 * LANES), np.float32)
    for d in range(2):
        for p in range(ML_PAIRS):
            for j in range(2):
                h = 2 * p + j
                for r in range(3):
                    a = _gate_lane(d, r, h)
                    g = _gate_lane(d, r + 3, h)
                    sel_a[a, d, p, j * LANES:(j + 1) * LANES] = 1
                    sel_a[a, d, p, 2 * LANES + j * ML_DK:2 * LANES + (j + 1) * ML_DK] = 1
                    sel_a[g, d, p, 3 * LANES + j * ML_DV:3 * LANES + (j + 1) * ML_DV] = 1
                    sel_m[a, d, p, j * ML_DK:(j + 1) * ML_DK] = 1
                    sel_d[a, d, p, j * 2 * LANES:(j + 1) * 2 * LANES] = 1
    as_bf = lambda m: jnp.asarray(m.reshape(LANES, -1), BF16)
    return as_bf(sel_a), as_bf(sel_m), as_bf(sel_d)


def _mlstm_kernel(u_ref, v_ref, so_ref, g_ref, cw_ref, wq_ref, wkt_ref, ghead_ref, sela_ref, selm_ref,
                  seld_ref, o_ref,
                  work, q_s, kt_s, gate_s, a_s, ct_s, wst_s, rows_s, mpb_s, dec_s, c_s):
    S = u_ref.shape[1]
    L = ML_CHUNK
    nc = S // L
    half = CONV_WIDTH // 2

    zpad = jnp.zeros((CONV_PAD, ML_WIDTH), F32)
    work[0:CONV_PAD, :] = zpad
    work[CONV_PAD + S:CONV_PAD + S + CONV_PAD, :] = zpad
    work[CONV_PAD:CONV_PAD + S, :] = u_ref[0]
    for r in range(nc):
        acc = jnp.zeros((L, ML_WIDTH), F32)
        for w in range(CONV_WIDTH):
            start = CONV_PAD - half + w + r * L
            acc = acc + work[start:start + L, :] * cw_ref[w:w + 1, :]
        c = (acc * _sigmoid(acc)).astype(BF16)
        q_s[r * L:(r + 1) * L, :] = _dot(c, wq_ref[...]).astype(BF16)
        kt_s[r] = _dot_nt(wkt_ref[...], c).astype(BF16)
    work[0:S, :] = jnp.zeros((S, ML_WIDTH), F32)
    c_s[...] = jnp.zeros(c_s.shape, F32)
    rows_s[...] = jnp.zeros(rows_s.shape, F32)

    def gate_lanes(nrows):
        lane = lax.broadcasted_iota(jnp.int32, (nrows, LANES), 1)
        fwd = lane < ML_DIR_LANES
        return fwd, jnp.right_shift(jnp.where(fwd, lane, lane - ML_DIR_LANES), 3)

    fwd_lane, rep = gate_lanes(L)
    fwd_row, _ = gate_lanes(1)
    tri_l = jnp.where(lax.broadcasted_iota(jnp.int32, (L, L), 0) >= lax.broadcasted_iota(jnp.int32, (L, L), 1),
                      1.0, 0.0).astype(BF16)
    btot, gmax = [], []
    for c in range(nc):
        rs = slice(c * L, (c + 1) * L)
        gf = g_ref[0, rs, LANES:2 * LANES]
        lf = jnp.minimum(gf, 0.0) - jnp.log1p(jnp.exp(-jnp.abs(gf)))
        bf = sum(_dot(tri_l, piece.astype(BF16)) for piece in _split3(lf))
        bt = bf[L - 1:L, :]
        bcum = jnp.where(fwd_lane, bf, bt - bf + lf)
        cc = g_ref[0, rs, 0:LANES] - bcum
        cmx = _cummax_dirs(cc, fwd_lane)
        gate_s[0, rs, :] = bcum
        gate_s[1, rs, :] = cc
        gate_s[2, rs, :] = cmx
        btot.append(bt)
        gmax.append(jnp.where(fwd_row, cmx[L - 1:L, :], cmx[0:1, :]))

    def scan_m(order):
        prev, new = [None] * nc, [None] * nc
        m = jnp.full((1, LANES), NEG_INIT, F32)
        for c in order:
            prev[c] = m
            m = jnp.maximum(btot[c] + m, btot[c] + gmax[c])
            new[c] = m
        return prev, new

    prev_f, new_f = scan_m(range(nc))
    prev_b, new_b = scan_m(reversed(range(nc)))

    for c in range(nc):
        rs = slice(c * L, (c + 1) * L)
        mprev = jnp.where(fwd_row, prev_f[c], prev_b[c])
        mnew = jnp.where(fwd_row, new_f[c], new_b[c])
        bcum, cc, cmx = gate_s[0, rs, :], gate_s[1, rs, :], gate_s[2, rs, :]
        mstab = jnp.maximum(mprev, cmx)
        a_s[rs, :] = _pieces((-mstab, -(bcum + mstab)), rep)
        ct_s[c] = cc.T
        wst_s[c] = jnp.exp(btot[c] + cc - mnew).T
        rows_s[0, c:c + 1, :] = mprev
        rows_s[1, c:c + 1, :] = jnp.exp(btot[c] + mprev - mnew)
    _, rep_c = gate_lanes(rows_s.shape[1])
    mpb = _dot(_pieces((rows_s[0],), rep_c), selm_ref[...])
    dec = _dot(_pieces((rows_s[1],), rep_c), seld_ref[...])
    for c in range(nc):
        mpb_s[c] = mpb[c:c + 1, :]
        dec_s[c] = dec[c:c + 1, :]

    def chunk_dir(ci, d):
        rows = pl.ds(pl.multiple_of(ci * L, L), L)
        t_i = lax.broadcasted_iota(jnp.int32, (L, L), 0)
        s_i = lax.broadcasted_iota(jnp.int32, (L, L), 1)
        allowed = t_i >= s_i if d == 0 else t_i <= s_i
        lane_ = lax.broadcasted_iota(jnp.int32, (L, LANES), 1)
        low = lane_ < ML_DV
        ones_e = jnp.where(low, 1.0, 0.0).astype(BF16)
        ones_o = jnp.where(low, 0.0, 1.0).astype(BF16)
        blk = 4 * LANES
        z = _dot(a_s[rows, :], sela_ref[:, d * ML_PAIRS * blk:(d + 1) * ML_PAIRS * blk])
        for p in range(ML_PAIRS):
            le = _gate_lane(d, 0, 2 * p)
            lo = le + 1
            dp = d * ML_PAIRS + p
            sl = slice(p * LANES, (p + 1) * LANES)
            zb = z[:, p * blk:(p + 1) * blk]
            qp = q_s[rows, sl]
            ktp = kt_s[ci, sl, :]
            vp = v_ref[0, rows, sl]
            zk = jnp.zeros((ML_DK, L), BF16)
            yt = jnp.concatenate([jnp.concatenate([ktp[:ML_DK], zk], axis=0),
                                  jnp.concatenate([zk, ktp[ML_DK:]], axis=0)], axis=1)
            qk = _dot(qp, yt)
            e_e = jnp.where(allowed, zb[:, 0:LANES] + ct_s[ci, le:le + 1, :], -jnp.inf)
            e_o = jnp.where(allowed, zb[:, LANES:2 * LANES] + ct_s[ci, lo:lo + 1, :], -jnp.inf)
            e_q = zb[:, 2 * LANES:3 * LANES] + mpb_s[ci, :, dp * LANES:(dp + 1) * LANES]
            pw = jnp.exp(jnp.concatenate([e_e, e_o, e_q], axis=1))
            x = (pw * jnp.concatenate([qk, qp.astype(F32)], axis=1)).astype(BF16)
            zero = jnp.zeros_like(vp)
            v2 = jnp.concatenate([jnp.concatenate([jnp.where(low, vp, zero), ones_e], axis=1),
                                  jnp.concatenate([jnp.where(low, zero, vp), ones_o], axis=1)], axis=0)
            cst = c_s[d, p]
            r = _dot(x, jnp.concatenate([v2, cst.astype(BF16)], axis=0))
            clamp = jnp.exp(zb[:, 3 * LANES:4 * LANES])
            work[rows, sl] = work[rows, sl] + r[:, 0:LANES] / jnp.maximum(jnp.abs(r[:, LANES:2 * LANES]), clamp)
            wsrow = jnp.concatenate([wst_s[ci, le:le + 1, :], wst_s[ci, lo:lo + 1, :]], axis=1)
            upd = _dot((yt.astype(F32) * wsrow).astype(BF16), v2)
            drow = dec_s[ci, :, dp * blk:(dp + 1) * blk]
            dmat = jnp.concatenate([jnp.broadcast_to(drow[:, 0:2 * LANES], (ML_DK, 2 * LANES)),
                                    jnp.broadcast_to(drow[:, 2 * LANES:4 * LANES], (ML_DK, 2 * LANES))], axis=0)
            c_s[d, p] = cst * dmat + upd

    def step(j, carry):
        chunk_dir(j, 0)
        chunk_dir(nc - 1 - j, 1)
        return carry

    lax.fori_loop(0, nc, step, 0)

    low = lax.broadcasted_iota(jnp.int32, (L, LANES), 1) < ML_DV
    for r in range(nc):
        rs = slice(r * L, (r + 1) * L)
        for p in range(ML_PAIRS):
            sl = slice(p * LANES, (p + 1) * LANES)
            hp = work[rs, sl]
            sq = hp * hp
            s_e = jnp.sum(jnp.where(low, sq, 0.0), axis=-1, keepdims=True)
            s_o = jnp.sum(jnp.where(low, 0.0, sq), axis=-1, keepdims=True)
            ms = jnp.where(low, s_e, s_o) * (1.0 / ML_DV)
            out = hp * lax.rsqrt(ms + EPS) * ghead_ref[:, sl] * so_ref[0, rs, sl].astype(F32)
            o_ref[0, rs, sl] = out.astype(BF16)


def _mlstm(u, v, so, g, conv_w, wq_bd, wkt_bd, g_head):
    B, S, W = u.shape
    L = ML_CHUNK
    nc = S // L
    sels = _mlstm_selectors()
    seq = lambda w: pl.BlockSpec((1, S, w), lambda b: (b, 0, 0))
    consts = (conv_w, wq_bd, wkt_bd, g_head) + sels
    return pl.pallas_call(
        _mlstm_kernel, grid=(B,),
        in_specs=[seq(W), seq(W), seq(W), seq(2 * LANES)] + [_const_spec(a.shape) for a in consts],
        out_specs=seq(W),
        out_shape=jax.ShapeDtypeStruct((B, S, W), BF16),
        scratch_shapes=[
            pltpu.VMEM((S + 2 * CONV_PAD, W), F32),
            pltpu.VMEM((S, W), BF16),
            pltpu.VMEM((nc, W, L), BF16),
            pltpu.VMEM((3, S, LANES), F32),
            pltpu.VMEM((S, LANES), BF16),
            pltpu.VMEM((nc, LANES, L), F32),
            pltpu.VMEM((nc, LANES, L), F32),
            pltpu.VMEM((2, -(-nc // 16) * 16, LANES), F32),
            pltpu.VMEM((nc, 1, 2 * ML_PAIRS * LANES), F32),
            pltpu.VMEM((nc, 1, 2 * ML_PAIRS * 4 * LANES), F32),
            pltpu.VMEM((2, ML_PAIRS, LANES, 2 * LANES), F32),
        ],
        compiler_params=pltpu.CompilerParams(dimension_semantics=("parallel",),
                                             vmem_limit_bytes=VMEM_LIMIT),
        name="mlstm",
    )(u, v, so, g, *consts)


def _memkv_kernel(mem_ref, g_ref, w_ref, k_ref, v_ref):
    mn = _rms(mem_ref[0], g_ref[...]).astype(BF16)
    kv = _dot(mn, w_ref[...])
    k_ref[0] = kv[:, :MEM_WIDTH].astype(BF16)
    v_ref[0] = kv[:, MEM_WIDTH:].astype(BF16)


def _memkv(mem, mem_g, w_kv):
    B, M, D = mem.shape
    blk = lambda w: pl.BlockSpec((1, M, w), lambda b: (b, 0, 0))
    return pl.pallas_call(
        _memkv_kernel, grid=(B,),
        in_specs=[blk(D), _const_spec(mem_g.shape), _const_spec(w_kv.shape)],
        out_specs=(blk(MEM_WIDTH), blk(MEM_WIDTH)),
        out_shape=(jax.ShapeDtypeStruct((B, M, MEM_WIDTH), BF16),) * 2,
        compiler_params=pltpu.CompilerParams(dimension_semantics=("parallel",)),
        name="memkv",
    )(mem, mem_g, w_kv)


def _memattn_kernel(q_ref, k_ref, v_ref, o_ref):
    scale = MEM_HEAD_DIM ** -0.5
    for hh in range(MEM_HEADS):
        sl = slice(hh * MEM_HEAD_DIM, (hh + 1) * MEM_HEAD_DIM)
        s = _dot_nt(q_ref[0, :, sl], k_ref[0, :, sl]) * scale
        e = jnp.exp(s - jnp.max(s, axis=-1, keepdims=True))
        den = jnp.sum(e, axis=-1, keepdims=True)
        o_ref[0, :, sl] = (_dot(e.astype(BF16), v_ref[0, :, sl]) / den).astype(BF16)


def _memattn(q, k, v, tq):
    B, S, W = q.shape
    M = k.shape[1]
    return pl.pallas_call(
        _memattn_kernel, grid=(B, S // tq),
        in_specs=[pl.BlockSpec((1, tq, W), lambda b, i: (b, i, 0)),
                  pl.BlockSpec((1, M, W), lambda b, i: (b, 0, 0)),
                  pl.BlockSpec((1, M, W), lambda b, i: (b, 0, 0))],
        out_specs=pl.BlockSpec((1, tq, W), lambda b, i: (b, i, 0)),
        out_shape=jax.ShapeDtypeStruct((B, S, W), BF16),
        compiler_params=pltpu.CompilerParams(dimension_semantics=("parallel", "parallel")),
        name="memattn",
    )(q, k, v)


def _out_kernel(x_ref, gates_ref, omla_ref, oml_ref, omem_ref, wb_ref, wout_ref, gffn_ref,
                wg_ref, wu_ref, wd_ref, gfin_ref, o_ref):
    merged = None
    for b, oref in enumerate((omla_ref, oml_ref, omem_ref)):
        y = _dot(oref[...], wb_ref[b]) * gates_ref[:, b * D_MODEL:(b + 1) * D_MODEL].astype(F32)
        merged = y if merged is None else merged + y
    x1 = x_ref[...] + _dot(merged.astype(BF16), wout_ref[...])
    h2 = _rms(x1, gffn_ref[...]).astype(BF16)
    hg = _dot(h2, wg_ref[...])
    a = (hg * _sigmoid(hg) * _dot(h2, wu_ref[...])).astype(BF16)
    x2 = x1 + _dot(a, wd_ref[...])
    o_ref[...] = _rms(x2, gfin_ref[...])


def _out_block(x, gates, o_mla, o_ml, o_mem, w_branch, w_out, g_ffn, w_g, w_u, w_d, g_final, tm):
    T, D = x.shape
    tok = lambda w: pl.BlockSpec((tm, w), lambda i: (i, 0))
    consts = (w_branch, w_out, g_ffn, w_g, w_u, w_d, g_final)
    return pl.pallas_call(
        _out_kernel, grid=(T // tm,),
        in_specs=[tok(D), tok(N_BRANCH * D), tok(o_mla.shape[1]), tok(o_ml.shape[1]), tok(o_mem.shape[1])]
        + [_const_spec(a.shape) for a in consts],
        out_specs=tok(D),
        out_shape=jax.ShapeDtypeStruct((T, D), x.dtype),
        compiler_params=pltpu.CompilerParams(dimension_semantics=("parallel",),
                                             vmem_limit_bytes=VMEM_LIMIT),
        name="out_block",
    )(x, gates, o_mla, o_ml, o_mem, *consts)


def _prep_in_weights(w_in, gate_bias):
    sizes = (N_BRANCH * D_MODEL, MLA_Q_RANK, MLA_KV_RANK, MLA_ROPE, ML_WIDTH, ML_WIDTH, ML_WIDTH,
             N_ML_GATES, MEM_WIDTH)
    offs = np.cumsum((0,) + sizes)
    w_gate, w_cq, w_ckv, w_kr, w_mu, w_mv, w_mo, w_mg, w_memq = [
        w_in[:, offs[i]:offs[i + 1]] for i in range(len(sizes))]
    half = MLA_ROPE // 2
    y1, y2 = w_kr[:, :half], w_kr[:, half:]
    z = lambda n: jnp.zeros((D_MODEL, n), w_in.dtype)
    w_kr = jnp.concatenate([z(MLA_NOPE), y1, y2, -y2, y1], axis=1)
    H = ML_HEADS
    pad = LANES - 2 * ML_DIR_LANES

    def gate_block(cols, fwd, bwd, zeros):
        return jnp.concatenate([cols(fwd)] * ML_REPL + [cols(bwd)] * ML_REPL + [zeros], axis=-1)

    wcols = lambda g: w_mg[:, g * H:(g + 1) * H]
    w_gi = gate_block(wcols, 0, 2, z(pad))
    w_gf = gate_block(wcols, 1, 3, z(pad))
    w_main = jnp.concatenate([w_gate, w_cq, w_ckv, w_kr, w_mu, w_mv, w_mo, w_memq, w_gi, w_gf],
                             axis=1).astype(BF16)
    bcols = lambda g: gate_bias[g]
    zb = jnp.zeros((pad,), gate_bias.dtype)
    gbias = jnp.concatenate([gate_block(bcols, 0, 2, zb), gate_block(bcols, 1, 3, zb)])[None, :]
    return w_main, gbias.astype(F32)


def _prep_mla_weights(w_uq, w_uk, w_uv):
    half = MLA_ROPE // 2
    qn = w_uq[:, :, :MLA_NOPE]
    x1 = w_uq[:, :, MLA_NOPE:MLA_NOPE + half]
    x2 = w_uq[:, :, MLA_NOPE + half:]
    wuq = jnp.concatenate([qn, x1, x2, -x2, x1], axis=-1).reshape(MLA_Q_RANK, MLA_HEADS * LANES)
    wuk = jnp.concatenate([w_uk, jnp.zeros_like(w_uk)], axis=-1).reshape(MLA_KV_RANK, MLA_HEADS * LANES).T
    wv = w_uv.reshape(MLA_KV_RANK, MLA_HEADS // 2, 2, MLA_V)
    zv = jnp.zeros_like(wv[:, :, 0])
    wuv = jnp.concatenate([wv[:, :, 0], zv, zv, wv[:, :, 1]], axis=-1).reshape(MLA_KV_RANK, MLA_HEADS * LANES)
    ones_lane = np.zeros((1, MLA_HEADS * LANES), np.float32)
    for hh in range(MLA_HEADS):
        ones_lane[0, hh * LANES + (MLA_V if hh % 2 == 0 else 0)] = 1.0
    return wuq.astype(BF16), wuk.astype(BF16), wuv.astype(BF16), jnp.asarray(ones_lane)


def _block_diag(w):
    H, d, _ = w.shape
    eye = jnp.eye(H, dtype=w.dtype)
    return (eye[:, None, :, None] * w[:, :, None, :]).reshape(H * d, H * d)


def _layer(x, mem, positions, g_mix, w_in, mla_g_q, mla_g_kv, mla_w_uq, mla_w_uk, mla_w_uv,
           ml_conv_w, ml_w_q, ml_w_k, ml_gate_bias, ml_g_head, mem_g, mem_w_kv,
           w_branch, w_out, g_ffn, w_ffn_gate, w_ffn_up, w_ffn_down, g_final):
    B, S, D = x.shape
    row = lambda g: g.reshape(1, -1).astype(F32)
    w_main, gbias = _prep_in_weights(w_in, ml_gate_bias)
    wuq, wuk, wuv, vones = _prep_mla_weights(mla_w_uq, mla_w_uk, mla_w_uv)
    half = MLA_ROPE // 2
    inv = ROPE_THETA ** (-jnp.arange(half, dtype=F32) / half)
    phase = np.where(np.arange(LANES) % MLA_NOPE < MLA_ROPE, 0.0, 0.5 * np.pi).astype(np.float32)
    invf = jnp.stack([jnp.tile(inv, LANES // half), jnp.asarray(phase)])

    gates, q, k, v, mu, mv, mo, mg, memq = _inproj(
        x, positions.reshape(B, S, 1), row(g_mix), w_main, gbias, row(mla_g_q), row(mla_g_kv),
        wuq, wuk, wuv, vones, invf, tm=min(512, S))
    o_mla = _mla_attention(q, k, v)
    o_ml = _mlstm(mu, mv, mo, mg, ml_conv_w.astype(F32), _block_diag(ml_w_q).astype(BF16),
                  _block_diag(ml_w_k * (ML_DK ** -0.5)).T.astype(BF16), row(ml_g_head))
    mk, mvv = _memkv(mem, row(mem_g), mem_w_kv.astype(BF16))
    o_mem = _memattn(memq, mk, mvv, tq=min(512, S))

    T = B * S
    flat = lambda a: a.reshape(T, a.shape[-1])
    out = _out_block(flat(x), flat(gates), flat(o_mla), flat(o_ml), flat(o_mem),
                     w_branch.astype(BF16), w_out.astype(BF16), row(g_ffn), w_ffn_gate.astype(BF16),
                     w_ffn_up.astype(BF16), w_ffn_down.astype(BF16), row(g_final), tm=min(512, T))
    return out.reshape(B, S, D)


def kernel(x, mem, positions, g_mix, w_in, mla_g_q, mla_g_kv, mla_w_uq, mla_w_uk, mla_w_uv, ml_conv_w,
           ml_w_q, ml_w_k, ml_gate_bias, ml_g_head, mem_g, mem_w_kv, w_branch, w_out, g_ffn,
           w_ffn_gate, w_ffn_up, w_ffn_down, g_final):
    depth = g_mix.shape[0]
    assert depth == 1, "the final norm is fused into the single layer's output kernel"
    return _layer(x, mem, positions, g_mix[0], w_in[0], mla_g_q[0], mla_g_kv[0], mla_w_uq[0],
                  mla_w_uk[0], mla_w_uv[0], ml_conv_w[0], ml_w_q[0], ml_w_k[0], ml_gate_bias[0],
                  ml_g_head[0], mem_g[0], mem_w_kv[0], w_branch[0], w_out[0], g_ffn[0],
                  w_ffn_gate[0], w_ffn_up[0], w_ffn_down[0], g_final)
```

```python
import math

import numpy as np
import jax
import jax.numpy as jnp
from jax import lax
from jax.experimental import pallas as pl
from jax.experimental.pallas import tpu as pltpu

D_MODEL = 1024
MEM_LEN = 256
EPS = 1e-6
MLA_HEADS = 8
MLA_NOPE = 64
MLA_ROPE = 32
MLA_V = 64
MLA_Q_RANK = 384
MLA_KV_RANK = 256
ROPE_THETA = 10000.0
ML_HEADS = 8
ML_DK = 64
ML_DV = 64
ML_WIDTH = ML_HEADS * ML_DV
CONV_WIDTH = 5
MEM_HEADS = 4
MEM_HEAD_DIM = 128
MEM_WIDTH = MEM_HEADS * MEM_HEAD_DIM
N_BRANCH = 3
N_ML_GATES = 4 * ML_HEADS
D_FF = 2816

LANES = 128
MLA_UNIT_ROWS = 512
MLA_KEY_CHUNK = 512
MLA_STEP_HEADS = 4
ML_CHUNK = LANES
ML_PAIRS = ML_HEADS // 2
ML_REPL = 6
ML_DIR_LANES = ML_REPL * ML_HEADS
NEG_INIT = -1e30
CONV_PAD = 8
VMEM_LIMIT = 56 * 1024 * 1024

BF16 = jnp.bfloat16
F32 = jnp.float32

_C_GATE = 0
_C_CQ = _C_GATE + N_BRANCH * D_MODEL
_C_CKV = _C_CQ + MLA_Q_RANK
_C_KR = _C_CKV + MLA_KV_RANK
_C_MU = _C_KR + LANES
_C_MV = _C_MU + ML_WIDTH
_C_MO = _C_MV + ML_WIDTH
_C_MEMQ = _C_MO + ML_WIDTH
_C_MGI = _C_MEMQ + MEM_WIDTH
_C_MGF = _C_MGI + LANES
_C_END = _C_MGF + LANES


def _const_spec(shape):
    nd = len(shape)
    return pl.BlockSpec(shape, lambda *_: (0,) * nd, pipeline_mode=pl.Buffered(1))


def _sigmoid(x):
    return 1.0 / (1.0 + jnp.exp(-x))


def _rms(x, g):
    return x * lax.rsqrt(jnp.mean(x * x, axis=-1, keepdims=True) + EPS) * g


def _dot(a, b):
    return jnp.dot(a, b, preferred_element_type=F32)


def _dot_nt(a, b):
    return lax.dot_general(a, b, (((1,), (1,)), ((), ())), preferred_element_type=F32)


def _inproj_kernel(x_ref, pos_ref, gmix_ref, w_ref, gbias_ref, gq_ref, gkv_ref, wuq_ref, wuk_ref,
                   wuv_ref, rope_ref,
                   gates_ref, q_ref, k_ref, v_ref, mu_ref, mv_ref, mo_ref, mg_ref, memq_ref):
    x = x_ref[0]
    tm = x.shape[0]
    h = _rms(x, gmix_ref[...]).astype(BF16)

    def proj(a, b):
        return _dot(h, w_ref[:, a:b])

    cqn = _rms(proj(_C_CQ, _C_CKV), gq_ref[...]).astype(BF16)
    ckvn = _rms(proj(_C_CKV, _C_KR), gkv_ref[...]).astype(BF16)
    qet = _dot_nt(wuq_ref[...], cqn)
    ke = _dot(ckvn, wuk_ref[...])
    vet = _dot_nt(wuv_ref[...], ckvn)

    mu_ref[0] = proj(_C_MU, _C_MV)
    mv_ref[0] = proj(_C_MV, _C_MO).astype(BF16)
    mo_ref[0] = _sigmoid(proj(_C_MO, _C_MEMQ)).astype(BF16)
    memq_ref[0] = proj(_C_MEMQ, _C_MGI).astype(BF16)
    mg_ref[0] = proj(_C_MGI, _C_END) + gbias_ref[...]

    lane = lax.broadcasted_iota(jnp.int32, (tm, LANES), 1)
    hm = tm // 2
    pos = pos_ref[0].astype(F32)
    pos2 = jnp.where(lax.broadcasted_iota(jnp.int32, (hm, LANES), 1) < MLA_NOPE, pos[:hm], pos[hm:])
    rope2 = jnp.cos(pos2 * rope_ref[0:1, :] - rope_ref[1:2, :])
    rope = jnp.concatenate([pltpu.roll(rope2, MLA_NOPE, 1), rope2], axis=0)
    kra = proj(_C_KR, _C_MU) * rope
    kr = jnp.where(lane < MLA_NOPE, 0.0,
                   kra + pltpu.roll(kra, MLA_ROPE, 1) + pltpu.roll(kra, LANES - MLA_ROPE, 1))
    q_scale = (MLA_NOPE + MLA_ROPE) ** -0.5 * math.log2(math.e)
    qmul_t = (jnp.where(lane < MLA_NOPE, 1.0, rope) * q_scale).T
    row = lax.broadcasted_iota(jnp.int32, (LANES, tm), 0)
    for hh in range(MLA_HEADS):
        sl = slice(hh * LANES, (hh + 1) * LANES)
        q_ref[0, hh] = (qet[sl, :] * qmul_t).astype(BF16)
        k_ref[0, hh] = (ke[:, sl] + kr).astype(BF16)
        v_ref[0, hh] = jnp.where(row == (MLA_V if hh % 2 == 0 else 0), 1.0, vet[sl, :]).astype(BF16)

    gates_ref[0] = _sigmoid(proj(_C_GATE, _C_CQ)).astype(BF16)


def _inproj(x, pos, g_mix, w_main, gbias, g_q, g_kv, wuq, wuk, wuv, invf, tm):
    B, S, D = x.shape
    grid = (B, S // tm)
    tok = lambda w: pl.BlockSpec((1, tm, w), lambda b, i: (b, i, 0))
    head = pl.BlockSpec((1, MLA_HEADS, tm, LANES), lambda b, i: (b, 0, i, 0))
    head_t = pl.BlockSpec((1, MLA_HEADS, LANES, tm), lambda b, i: (b, 0, 0, i))
    out_shape = (
        jax.ShapeDtypeStruct((B, S, N_BRANCH * D_MODEL), BF16),
        jax.ShapeDtypeStruct((B, MLA_HEADS, LANES, S), BF16),
        jax.ShapeDtypeStruct((B, MLA_HEADS, S, LANES), BF16),
        jax.ShapeDtypeStruct((B, MLA_HEADS, LANES, S), BF16),
        jax.ShapeDtypeStruct((B, S, ML_WIDTH), F32),
        jax.ShapeDtypeStruct((B, S, ML_WIDTH), BF16),
        jax.ShapeDtypeStruct((B, S, ML_WIDTH), BF16),
        jax.ShapeDtypeStruct((B, S, 2 * LANES), F32),
        jax.ShapeDtypeStruct((B, S, MEM_WIDTH), BF16),
    )
    out_specs = (tok(N_BRANCH * D_MODEL), head_t, head, head_t, tok(ML_WIDTH), tok(ML_WIDTH),
                 tok(ML_WIDTH), tok(2 * LANES), tok(MEM_WIDTH))
    consts = (g_mix, w_main, gbias, g_q, g_kv, wuq, wuk, wuv, invf)
    in_specs = [tok(D), pl.BlockSpec((1, tm, 1), lambda b, i: (b, i, 0))] + [_const_spec(a.shape) for a in consts]
    return pl.pallas_call(
        _inproj_kernel, grid=grid, in_specs=in_specs, out_specs=out_specs, out_shape=out_shape,
        compiler_params=pltpu.CompilerParams(dimension_semantics=("parallel", "parallel"),
                                             vmem_limit_bytes=VMEM_LIMIT),
        name="inproj",
    )(x, pos, *consts)


def _mla_kernel(qt_ref, k_ref, vt_ref, o_ref, s_buf):
    S = k_ref.shape[2]
    tu = s_buf.shape[2]
    low = lax.broadcasted_iota(jnp.int32, (LANES, tu), 0) < MLA_V
    units = [(pair, t, j) for pair in range(MLA_STEP_HEADS // 2) for t in range(S // tu) for j in range(2)]

    def scores(u):
        pair, t, j = units[u]
        hh = 2 * pair + j
        s_buf[u % 2] = _dot(k_ref[0, hh], qt_ref[0, hh, :, t * tu:(t + 1) * tu])

    chunks = [slice(kc, min(kc + MLA_KEY_CHUNK, S)) for kc in range(0, S, MLA_KEY_CHUNK)]
    scores(0)
    outs = []
    for u, (pair, t, j) in enumerate(units):
        if u + 1 < len(units):
            scores(u + 1)
        hh = 2 * pair + j
        m = jnp.max(s_buf[u % 2], axis=0, keepdims=True)
        r = None
        for ks in chunks:
            p = jnp.exp2(s_buf[u % 2, ks, :] - m).astype(BF16)
            part = _dot(vt_ref[0, hh, :, ks], p)
            r = part if r is None else r + part
        outs.append(r / (r[MLA_V:MLA_V + 1, :] if j == 0 else r[0:1, :]))
        if j == 1:
            o_ref[0, t * tu:(t + 1) * tu, pair * LANES:(pair + 1) * LANES] = (
                jnp.where(low, outs[0], outs[1]).T.astype(BF16))
            outs = []


def _mla_attention(qt, k, vt):
    B, H, S, _ = k.shape
    hs = MLA_STEP_HEADS
    tu = min(MLA_UNIT_ROWS, S)
    heads = lambda shape: pl.BlockSpec((1, hs) + shape, lambda b, p: (b, p, 0, 0))
    return pl.pallas_call(
        _mla_kernel, grid=(B, H // hs),
        in_specs=[heads((LANES, S)), heads((S, LANES)), heads((LANES, S))],
        out_specs=pl.BlockSpec((1, S, hs * MLA_V), lambda b, p: (b, 0, p)),
        out_shape=jax.ShapeDtypeStruct((B, S, H * MLA_V), BF16),
        scratch_shapes=[pltpu.VMEM((2, S, tu), F32)],
        compiler_params=pltpu.CompilerParams(dimension_semantics=("parallel", "parallel"),
                                             vmem_limit_bytes=VMEM_LIMIT),
        name="mla_attn",
    )(qt, k, vt)


def _gate_lane(d, r, h):
    return ML_DIR_LANES * d + ML_HEADS * r + h


def _split3(x):
    hi = x.astype(BF16).astype(F32)
    r = x - hi
    mid = r.astype(BF16).astype(F32)
    lo = (r - mid).astype(BF16).astype(F32)
    return hi, mid, lo


def _pieces(srcs, rep):
    out = None
    for i, src in enumerate(srcs):
        for j, piece in enumerate(_split3(src)):
            out = piece if out is None else jnp.where(rep == 3 * i + j, piece, out)
    return out.astype(BF16)


def _cummax_dirs(x, fwd_lane):
    n = x.shape[0]
    row = lax.broadcasted_iota(jnp.int32, x.shape, 0)
    sh = 1
    while sh < n:
        up = jnp.where(row >= sh, pltpu.roll(x, sh, 0), -jnp.inf)
        dn = jnp.where(row < n - sh, pltpu.roll(x, n - sh, 0), -jnp.inf)
        x = jnp.maximum(x, jnp.where(fwd_lane, up, dn))
        sh *= 2
    return x


def _mlstm_selectors():
    sel_a = np.zeros((LANES, 2, ML_PAIRS, 4 * LANES), np.float32)
    sel_m = np.zeros((LANES, 2, ML_PAIRS, LANES), np.float32)
    sel_d = np.zeros((LANES, 2, ML_PAIRS, 4 * LANES), np.float32)
    for d in range(2):
        for p in range(ML_PAIRS):
            for j in range(2):
                h = 2 * p + j
                for r in range(3):
                    a = _gate_lane(d, r, h)
                    g = _gate_lane(d, r + 3, h)
                    sel_a[a, d, p, j * LANES:(j + 1) * LANES] = 1
                    sel_a[a, d, p, 2 * LANES + j * ML_DK:2 * LANES + (j + 1) * ML_DK] = 1
                    sel_a[g, d, p, 3 * LANES + j * ML_DV:3 * LANES + (j + 1) * ML_DV] = 1
                    sel_m[a, d, p, j * ML_DK:(j + 1) * ML_DK] = 1
                    sel_d[a, d, p, j * 2 * LANES:(j + 1) * 2 * LANES] = 1
    as_bf = lambda m: jnp.asarray(m.reshape(LANES, -1), BF16)
    return as_bf(sel_a), as_bf(sel_m), as_bf(sel_d)


def _mlstm_kernel(u_ref, v_ref, so_ref, g_ref, cw_ref, wq_ref, wkt_ref, ghead_ref, sela_ref, selm_ref,
                  seld_ref, o_ref,
                  upad, h_s, q_s, kt_s, gate_s, a_s, ct_s, wst_s, rows_s, mpb_s, dec_s, c_s):
    S = u_ref.shape[1]
    L = ML_CHUNK
    nc = S // L
    half = CONV_WIDTH // 2

    zpad = jnp.zeros((CONV_PAD, ML_WIDTH), F32)
    upad[0:CONV_PAD, :] = zpad
    upad[CONV_PAD + S:CONV_PAD + S + CONV_PAD, :] = zpad
    upad[CONV_PAD:CONV_PAD + S, :] = u_ref[0]
    c_s[...] = jnp.zeros(c_s.shape, F32)
    rows_s[...] = jnp.zeros(rows_s.shape, F32)

    def conv_chunk(r):
        acc = jnp.zeros((L, ML_WIDTH), F32)
        for w in range(CONV_WIDTH):
            start = CONV_PAD - half + w + r * L
            acc = acc + upad[start:start + L, :] * cw_ref[w:w + 1, :]
        c = (acc * _sigmoid(acc)).astype(BF16)
        q_s[r * L:(r + 1) * L, :] = _dot(c, wq_ref[...]).astype(BF16)
        kt_s[r] = _dot_nt(wkt_ref[...], c).astype(BF16)

    def gate_lanes(nrows):
        lane = lax.broadcasted_iota(jnp.int32, (nrows, LANES), 1)
        fwd = lane < ML_DIR_LANES
        return fwd, jnp.right_shift(jnp.where(fwd, lane, lane - ML_DIR_LANES), 3)

    fwd_lane, rep = gate_lanes(L)
    fwd_row, _ = gate_lanes(1)
    tri_l = jnp.where(lax.broadcasted_iota(jnp.int32, (L, L), 0) >= lax.broadcasted_iota(jnp.int32, (L, L), 1),
                      1.0, 0.0).astype(BF16)
    btot, gmax = [], []
    for c in range(nc):
        rs = slice(c * L, (c + 1) * L)
        gf = g_ref[0, rs, LANES:2 * LANES]
        lf = jnp.minimum(gf, 0.0) - jnp.log1p(jnp.exp(-jnp.abs(gf)))
        bf = sum(_dot(tri_l, piece.astype(BF16)) for piece in _split3(lf))
        bt = bf[L - 1:L, :]
        bcum = jnp.where(fwd_lane, bf, bt - bf + lf)
        cc = g_ref[0, rs, 0:LANES] - bcum
        cmx = _cummax_dirs(cc, fwd_lane)
        gate_s[0, rs, :] = bcum
        gate_s[1, rs, :] = cc
        gate_s[2, rs, :] = cmx
        btot.append(bt)
        gmax.append(jnp.where(fwd_row, cmx[L - 1:L, :], cmx[0:1, :]))

    def scan_m(order):
        prev, new = [None] * nc, [None] * nc
        m = jnp.full((1, LANES), NEG_INIT, F32)
        for c in order:
            prev[c] = m
            m = jnp.maximum(btot[c] + m, btot[c] + gmax[c])
            new[c] = m
        return prev, new

    prev_f, new_f = scan_m(range(nc))
    prev_b, new_b = scan_m(reversed(range(nc)))

    for c in range(nc):
        rs = slice(c * L, (c + 1) * L)
        mprev = jnp.where(fwd_row, prev_f[c], prev_b[c])
        mnew = jnp.where(fwd_row, new_f[c], new_b[c])
        bcum, cc, cmx = gate_s[0, rs, :], gate_s[1, rs, :], gate_s[2, rs, :]
        mstab = jnp.maximum(mprev, cmx)
        a_s[rs, :] = _pieces((-mstab, -(bcum + mstab)), rep)
        ct_s[c] = cc.T
        wst_s[c] = jnp.exp(btot[c] + cc - mnew).T
        rows_s[0, c:c + 1, :] = mprev
        rows_s[1, c:c + 1, :] = jnp.exp(btot[c] + mprev - mnew)
    _, rep_c = gate_lanes(rows_s.shape[1])
    mpb = _dot(_pieces((rows_s[0],), rep_c), selm_ref[...])
    dec = _dot(_pieces((rows_s[1],), rep_c), seld_ref[...])
    for c in range(nc):
        mpb_s[c] = mpb[c:c + 1, :]
        dec_s[c] = dec[c:c + 1, :]

    t_i = lax.broadcasted_iota(jnp.int32, (L, L), 0)
    s_i = lax.broadcasted_iota(jnp.int32, (L, L), 1)
    low = lax.broadcasted_iota(jnp.int32, (L, LANES), 1) < ML_DV
    ones_e = jnp.where(low, 1.0, 0.0).astype(BF16)
    ones_o = jnp.where(low, 0.0, 1.0).astype(BF16)
    blk = 4 * LANES

    def finish(ci, p, hp):
        rows = slice(ci * L, (ci + 1) * L)
        sl = slice(p * LANES, (p + 1) * LANES)
        sq = hp * hp
        s_e = jnp.sum(jnp.where(low, sq, 0.0), axis=-1, keepdims=True)
        s_o = jnp.sum(jnp.where(low, 0.0, sq), axis=-1, keepdims=True)
        ms = jnp.where(low, s_e, s_o) * (1.0 / ML_DV)
        out = hp * lax.rsqrt(ms + EPS) * ghead_ref[:, sl] * so_ref[0, rows, sl].astype(F32)
        o_ref[0, rows, sl] = out.astype(BF16)

    def chunk_dir(ci, d, first):
        rows = slice(ci * L, (ci + 1) * L)
        allowed = t_i >= s_i if d == 0 else t_i <= s_i
        z = _dot(a_s[rows, :], sela_ref[:, d * ML_PAIRS * blk:(d + 1) * ML_PAIRS * blk])
        for p in range(ML_PAIRS):
            le = _gate_lane(d, 0, 2 * p)
            lo = le + 1
            dp = d * ML_PAIRS + p
            sl = slice(p * LANES, (p + 1) * LANES)
            zb = z[:, p * blk:(p + 1) * blk]
            qp = q_s[rows, sl]
            ktp = kt_s[ci, sl, :]
            vp = v_ref[0, rows, sl]
            zk = jnp.zeros((ML_DK, L), BF16)
            yt = jnp.concatenate([jnp.concatenate([ktp[:ML_DK], zk], axis=0),
                                  jnp.concatenate([zk, ktp[ML_DK:]], axis=0)], axis=1)
            qk = _dot(qp, yt)
            e_e = jnp.where(allowed, zb[:, 0:LANES] + ct_s[ci, le:le + 1, :], -jnp.inf)
            e_o = jnp.where(allowed, zb[:, LANES:2 * LANES] + ct_s[ci, lo:lo + 1, :], -jnp.inf)
            e_q = zb[:, 2 * LANES:3 * LANES] + mpb_s[ci, :, dp * LANES:(dp + 1) * LANES]
            pw = jnp.exp(jnp.concatenate([e_e, e_o, e_q], axis=1))
            x = (pw * jnp.concatenate([qk, qp.astype(F32)], axis=1)).astype(BF16)
            zero = jnp.zeros_like(vp)
            v2 = jnp.concatenate([jnp.concatenate([jnp.where(low, vp, zero), ones_e], axis=1),
                                  jnp.concatenate([jnp.where(low, zero, vp), ones_o], axis=1)], axis=0)
            cst = c_s[d, p]
            r = _dot(x, jnp.concatenate([v2, cst.astype(BF16)], axis=0))
            clamp = jnp.exp(zb[:, 3 * LANES:4 * LANES])
            hdir = r[:, 0:LANES] / jnp.maximum(jnp.abs(r[:, LANES:2 * LANES]), clamp)
            if first:
                h_s[rows, sl] = hdir
            else:
                finish(ci, p, h_s[rows, sl] + hdir)
            wsrow = jnp.concatenate([wst_s[ci, le:le + 1, :], wst_s[ci, lo:lo + 1, :]], axis=1)
            upd = _dot((yt.astype(F32) * wsrow).astype(BF16), v2)
            drow = dec_s[ci, :, dp * blk:(dp + 1) * blk]
            dmat = jnp.concatenate([jnp.broadcast_to(drow[:, 0:2 * LANES], (ML_DK, 2 * LANES)),
                                    jnp.broadcast_to(drow[:, 2 * LANES:4 * LANES], (ML_DK, 2 * LANES))], axis=0)
            c_s[d, p] = cst * dmat + upd

    conv_chunk(0)
    conv_chunk(nc - 1)
    for j in range(nc):
        first = j < nc - 1 - j
        chunk_dir(j, 0, first)
        chunk_dir(nc - 1 - j, 1, first)
        if j + 1 < nc - 2 - j:
            conv_chunk(j + 1)
            conv_chunk(nc - 2 - j)


def _mlstm(u, v, so, g, conv_w, wq_bd, wkt_bd, g_head):
    B, S, W = u.shape
    L = ML_CHUNK
    nc = S // L
    sels = _mlstm_selectors()
    seq = lambda w: pl.BlockSpec((1, S, w), lambda b: (b, 0, 0))
    consts = (conv_w, wq_bd, wkt_bd, g_head) + sels
    return pl.pallas_call(
        _mlstm_kernel, grid=(B,),
        in_specs=[seq(W), seq(W), seq(W), seq(2 * LANES)] + [_const_spec(a.shape) for a in consts],
        out_specs=seq(W),
        out_shape=jax.ShapeDtypeStruct((B, S, W), BF16),
        scratch_shapes=[
            pltpu.VMEM((S + 2 * CONV_PAD, W), F32),
            pltpu.VMEM((S, W), F32),
            pltpu.VMEM((S, W), BF16),
            pltpu.VMEM((nc, W, L), BF16),
            pltpu.VMEM((3, S, LANES), F32),
            pltpu.VMEM((S, LANES), BF16),
            pltpu.VMEM((nc, LANES, L), F32),
            pltpu.VMEM((nc, LANES, L), F32),
            pltpu.VMEM((2, -(-nc // 16) * 16, LANES), F32),
            pltpu.VMEM((nc, 1, 2 * ML_PAIRS * LANES), F32),
            pltpu.VMEM((nc, 1, 2 * ML_PAIRS * 4 * LANES), F32),
            pltpu.VMEM((2, ML_PAIRS, LANES, 2 * LANES), F32),
        ],
        compiler_params=pltpu.CompilerParams(dimension_semantics=("parallel",),
                                             vmem_limit_bytes=VMEM_LIMIT),
        name="mlstm",
    )(u, v, so, g, *consts)


def _memkv_kernel(mem_ref, g_ref, w_ref, k_ref, v_ref):
    mn = _rms(mem_ref[0], g_ref[...]).astype(BF16)
    kv = _dot(mn, w_ref[...])
    k_ref[0] = kv[:, :MEM_WIDTH].astype(BF16)
    v_ref[0] = kv[:, MEM_WIDTH:].astype(BF16)


def _memkv(mem, mem_g, w_kv):
    B, M, D = mem.shape
    blk = lambda w: pl.BlockSpec((1, M, w), lambda b: (b, 0, 0))
    return pl.pallas_call(
        _memkv_kernel, grid=(B,),
        in_specs=[blk(D), _const_spec(mem_g.shape), _const_spec(w_kv.shape)],
        out_specs=(blk(MEM_WIDTH), blk(MEM_WIDTH)),
        out_shape=(jax.ShapeDtypeStruct((B, M, MEM_WIDTH), BF16),) * 2,
        compiler_params=pltpu.CompilerParams(dimension_semantics=("parallel",)),
        name="memkv",
    )(mem, mem_g, w_kv)


def _memattn_kernel(q_ref, k_ref, v_ref, o_ref):
    scale = MEM_HEAD_DIM ** -0.5
    for hh in range(MEM_HEADS):
        sl = slice(hh * MEM_HEAD_DIM, (hh + 1) * MEM_HEAD_DIM)
        s = _dot_nt(q_ref[0, :, sl], k_ref[0, :, sl]) * scale
        e = jnp.exp(s - jnp.max(s, axis=-1, keepdims=True))
        den = jnp.sum(e, axis=-1, keepdims=True)
        o_ref[0, :, sl] = (_dot(e.astype(BF16), v_ref[0, :, sl]) / den).astype(BF16)


def _memattn(q, k, v, tq):
    B, S, W = q.shape
    M = k.shape[1]
    return pl.pallas_call(
        _memattn_kernel, grid=(B, S // tq),
        in_specs=[pl.BlockSpec((1, tq, W), lambda b, i: (b, i, 0)),
                  pl.BlockSpec((1, M, W), lambda b, i: (b, 0, 0)),
                  pl.BlockSpec((1, M, W), lambda b, i: (b, 0, 0))],
        out_specs=pl.BlockSpec((1, tq, W), lambda b, i: (b, i, 0)),
        out_shape=jax.ShapeDtypeStruct((B, S, W), BF16),
        compiler_params=pltpu.CompilerParams(dimension_semantics=("parallel", "parallel")),
        name="memattn",
    )(q, k, v)


def _out_kernel(x_ref, gates_ref, omla_ref, oml_ref, omem_ref, wb_ref, wout_ref, gffn_ref,
                wg_ref, wu_ref, wd_ref, gfin_ref, o_ref):
    merged = None
    for b, oref in enumerate((omla_ref, oml_ref, omem_ref)):
        y = _dot(oref[...], wb_ref[b]) * gates_ref[:, b * D_MODEL:(b + 1) * D_MODEL].astype(F32)
        merged = y if merged is None else merged + y
    x1 = x_ref[...] + _dot(merged.astype(BF16), wout_ref[...])
    h2 = _rms(x1, gffn_ref[...]).astype(BF16)
    hg = _dot(h2, wg_ref[...])
    a = (hg * _sigmoid(hg) * _dot(h2, wu_ref[...])).astype(BF16)
    x2 = x1 + _dot(a, wd_ref[...])
    o_ref[...] = _rms(x2, gfin_ref[...])


def _out_block(x, gates, o_mla, o_ml, o_mem, w_branch, w_out, g_ffn, w_g, w_u, w_d, g_final, tm):
    T, D = x.shape
    tok = lambda w: pl.BlockSpec((tm, w), lambda i: (i, 0))
    consts = (w_branch, w_out, g_ffn, w_g, w_u, w_d, g_final)
    return pl.pallas_call(
        _out_kernel, grid=(T // tm,),
        in_specs=[tok(D), tok(N_BRANCH * D), tok(o_mla.shape[1]), tok(o_ml.shape[1]), tok(o_mem.shape[1])]
        + [_const_spec(a.shape) for a in consts],
        out_specs=tok(D),
        out_shape=jax.ShapeDtypeStruct((T, D), x.dtype),
        compiler_params=pltpu.CompilerParams(dimension_semantics=("parallel",),
                                             vmem_limit_bytes=VMEM_LIMIT),
        name="out_block",
    )(x, gates, o_mla, o_ml, o_mem, *consts)


def _prep_in_weights(w_in, gate_bias):
    sizes = (N_BRANCH * D_MODEL, MLA_Q_RANK, MLA_KV_RANK, MLA_ROPE, ML_WIDTH, ML_WIDTH, ML_WIDTH,
             N_ML_GATES, MEM_WIDTH)
    offs = np.cumsum((0,) + sizes)
    w_gate, w_cq, w_ckv, w_kr, w_mu, w_mv, w_mo, w_mg, w_memq = [
        w_in[:, offs[i]:offs[i + 1]] for i in range(len(sizes))]
    half = MLA_ROPE // 2
    y1, y2 = w_kr[:, :half], w_kr[:, half:]
    z = lambda n: jnp.zeros((D_MODEL, n), w_in.dtype)
    w_kr = jnp.concatenate([z(MLA_NOPE), y1, y2, -y2, y1], axis=1)
    H = ML_HEADS
    pad = LANES - 2 * ML_DIR_LANES

    def gate_block(cols, fwd, bwd, zeros):
        return jnp.concatenate([cols(fwd)] * ML_REPL + [cols(bwd)] * ML_REPL + [zeros], axis=-1)

    wcols = lambda g: w_mg[:, g * H:(g + 1) * H]
    w_gi = gate_block(wcols, 0, 2, z(pad))
    w_gf = gate_block(wcols, 1, 3, z(pad))
    w_main = jnp.concatenate([w_gate, w_cq, w_ckv, w_kr, w_mu, w_mv, w_mo, w_memq, w_gi, w_gf],
                             axis=1).astype(BF16)
    bcols = lambda g: gate_bias[g]
    zb = jnp.zeros((pad,), gate_bias.dtype)
    gbias = jnp.concatenate([gate_block(bcols, 0, 2, zb), gate_block(bcols, 1, 3, zb)])[None, :]
    return w_main, gbias.astype(F32)


def _prep_mla_weights(w_uq, w_uk, w_uv):
    half = MLA_ROPE // 2
    qn = w_uq[:, :, :MLA_NOPE]
    x1 = w_uq[:, :, MLA_NOPE:MLA_NOPE + half]
    x2 = w_uq[:, :, MLA_NOPE + half:]
    wuq = jnp.concatenate([qn, x1, x2, -x2, x1], axis=-1).reshape(MLA_Q_RANK, MLA_HEADS * LANES).T
    wuk = jnp.concatenate([w_uk, jnp.zeros_like(w_uk)], axis=-1).reshape(MLA_KV_RANK, MLA_HEADS * LANES)
    wv = w_uv.reshape(MLA_KV_RANK, MLA_HEADS // 2, 2, MLA_V)
    zv = jnp.zeros_like(wv[:, :, 0])
    wuv = jnp.concatenate([wv[:, :, 0], zv, zv, wv[:, :, 1]], axis=-1).reshape(MLA_KV_RANK, MLA_HEADS * LANES).T
    return wuq.astype(BF16), wuk.astype(BF16), wuv.astype(BF16)


def _block_diag(w):
    H, d, _ = w.shape
    eye = jnp.eye(H, dtype=w.dtype)
    return (eye[:, None, :, None] * w[:, :, None, :]).reshape(H * d, H * d)


def _layer(x, mem, positions, g_mix, w_in, mla_g_q, mla_g_kv, mla_w_uq, mla_w_uk, mla_w_uv,
           ml_conv_w, ml_w_q, ml_w_k, ml_gate_bias, ml_g_head, mem_g, mem_w_kv,
           w_branch, w_out, g_ffn, w_ffn_gate, w_ffn_up, w_ffn_down, g_final):
    B, S, D = x.shape
    row = lambda g: g.reshape(1, -1).astype(F32)
    w_main, gbias = _prep_in_weights(w_in, ml_gate_bias)
    wuq, wuk, wuv = _prep_mla_weights(mla_w_uq, mla_w_uk, mla_w_uv)
    half = MLA_ROPE // 2
    inv = ROPE_THETA ** (-jnp.arange(half, dtype=F32) / half)
    phase = np.where(np.arange(LANES) % MLA_NOPE < MLA_ROPE, 0.0, 0.5 * np.pi).astype(np.float32)
    invf = jnp.stack([jnp.tile(inv, LANES // half), jnp.asarray(phase)])

    gates, q, k, v, mu, mv, mo, mg, memq = _inproj(
        x, positions.reshape(B, S, 1), row(g_mix), w_main, gbias, row(mla_g_q), row(mla_g_kv),
        wuq, wuk, wuv, invf, tm=min(512, S))
    o_mla = _mla_attention(q, k, v)
    o_ml = _mlstm(mu, mv, mo, mg, ml_conv_w.astype(F32), _block_diag(ml_w_q).astype(BF16),
                  _block_diag(ml_w_k * (ML_DK ** -0.5)).T.astype(BF16), row(ml_g_head))
    mk, mvv = _memkv(mem, row(mem_g), mem_w_kv.astype(BF16))
    o_mem = _memattn(memq, mk, mvv, tq=min(512, S))

    T = B * S
    flat = lambda a: a.reshape(T, a.shape[-1])
    out = _out_block(flat(x), flat(gates), flat(o_mla), flat(o_ml), flat(o_mem),
                     w_branch.astype(BF16), w_out.astype(BF16), row(g_ffn), w_ffn_gate.astype(BF16),
                     w_ffn_up.astype(BF16), w_ffn_down.astype(BF16), row(g_final), tm=min(512, T))
    return out.reshape(B, S, D)


def kernel(x, mem, positions, g_mix, w_in, mla_g_q, mla_g_kv, mla_w_uq, mla_w_uk, mla_w_uv, ml_conv_w,
           ml_w_q, ml_w_k, ml_gate_bias, ml_g_head, mem_g, mem_w_kv, w_branch, w_out, g_ffn,
           w_ffn_gate, w_ffn_up, w_ffn_down, g_final):
    depth = g_mix.shape[0]
    assert depth == 1, "the final norm is fused into the single layer's output kernel"
    return _layer(x, mem, positions, g_mix[0], w_in[0], mla_g_q[0], mla_g_kv[0], mla_w_uq[0],
                  mla_w_uk[0], mla_w_uv[0], ml_conv_w[0], ml_w_q[0], ml_w_k[0], ml_gate_bias[0],
                  ml_g_head[0], mem_g[0], mem_w_kv[0], w_branch[0], w_out[0], g_ffn[0],
                  w_ffn_gate[0], w_ffn_up[0], w_ffn_down[0], g_final)
```

```python
import math

import numpy as np
import jax
import jax.numpy as jnp
from jax import lax
from jax.experimental import pallas as pl
from jax.experimental.pallas import tpu as pltpu

D_MODEL = 1024
MEM_LEN = 256
EPS = 1e-6
MLA_HEADS = 8
MLA_NOPE = 64
MLA_ROPE = 32
MLA_V = 64
MLA_Q_RANK = 384
MLA_KV_RANK = 256
ROPE_THETA = 10000.0
ML_HEADS = 8
ML_DK = 64
ML_DV = 64
ML_WIDTH = ML_HEADS * ML_DV
CONV_WIDTH = 5
MEM_HEADS = 4
MEM_HEAD_DIM = 128
MEM_WIDTH = MEM_HEADS * MEM_HEAD_DIM
N_BRANCH = 3
N_ML_GATES = 4 * ML_HEADS
D_FF = 2816

LANES = 128
MLA_UNIT_ROWS = 512
MLA_STEP_HEADS = 4
ML_CHUNK = LANES
ML_PAIRS = ML_HEADS // 2
ML_REPL = 6
ML_DIR_LANES = ML_REPL * ML_HEADS
NEG_INIT = -1e30
CONV_PAD = 8
VMEM_LIMIT = 56 * 1024 * 1024

BF16 = jnp.bfloat16
F32 = jnp.float32

_C_GATE = 0
_C_CQ = _C_GATE + N_BRANCH * D_MODEL
_C_CKV = _C_CQ + MLA_Q_RANK
_C_KR = _C_CKV + MLA_KV_RANK
_C_MU = _C_KR + LANES
_C_MV = _C_MU + ML_WIDTH
_C_MO = _C_MV + ML_WIDTH
_C_MEMQ = _C_MO + ML_WIDTH
_C_MGI = _C_MEMQ + MEM_WIDTH
_C_MGF = _C_MGI + LANES
_C_END = _C_MGF + LANES


def _const_spec(shape):
    nd = len(shape)
    return pl.BlockSpec(shape, lambda *_: (0,) * nd, pipeline_mode=pl.Buffered(1))


def _sigmoid(x):
    return 1.0 / (1.0 + jnp.exp(-x))


def _rms(x, g):
    return x * lax.rsqrt(jnp.mean(x * x, axis=-1, keepdims=True) + EPS) * g


def _dot(a, b):
    return jnp.dot(a, b, preferred_element_type=F32)


def _dot_nt(a, b):
    return lax.dot_general(a, b, (((1,), (1,)), ((), ())), preferred_element_type=F32)


def _inproj_kernel(x_ref, pos_ref, gmix_ref, w_ref, gbias_ref, gq_ref, gkv_ref, wuq_ref, wuk_ref,
                   wuv_ref, rope_ref,
                   gates_ref, q_ref, k_ref, v_ref, mu_ref, mv_ref, mo_ref, mg_ref, memq_ref):
    x = x_ref[0]
    tm = x.shape[0]
    h = _rms(x, gmix_ref[...]).astype(BF16)

    def proj(a, b):
        return _dot(h, w_ref[:, a:b])

    cqn = _rms(proj(_C_CQ, _C_CKV), gq_ref[...]).astype(BF16)
    ckvn = _rms(proj(_C_CKV, _C_KR), gkv_ref[...]).astype(BF16)
    qe = _dot(cqn, wuq_ref[...])
    ket = _dot_nt(wuk_ref[...], ckvn)
    ve = _dot(ckvn, wuv_ref[...])

    mu_ref[0] = proj(_C_MU, _C_MV)
    mv_ref[0] = proj(_C_MV, _C_MO).astype(BF16)
    mo_ref[0] = _sigmoid(proj(_C_MO, _C_MEMQ)).astype(BF16)
    memq_ref[0] = proj(_C_MEMQ, _C_MGI).astype(BF16)
    mg_ref[0] = proj(_C_MGI, _C_END) + gbias_ref[...]

    lane = lax.broadcasted_iota(jnp.int32, (tm, LANES), 1)
    hm = tm // 2
    pos = pos_ref[0].astype(F32)
    pos2 = jnp.where(lax.broadcasted_iota(jnp.int32, (hm, LANES), 1) < MLA_NOPE, pos[:hm], pos[hm:])
    rope2 = jnp.cos(pos2 * rope_ref[0:1, :] - rope_ref[1:2, :])
    rope = jnp.concatenate([pltpu.roll(rope2, MLA_NOPE, 1), rope2], axis=0)
    kra = proj(_C_KR, _C_MU) * rope
    kr = jnp.where(lane < MLA_NOPE, 0.0,
                   kra + pltpu.roll(kra, MLA_ROPE, 1) + pltpu.roll(kra, LANES - MLA_ROPE, 1))
    q_scale = (MLA_NOPE + MLA_ROPE) ** -0.5 * math.log2(math.e)
    qmul = jnp.where(lane < MLA_NOPE, 1.0, rope) * q_scale
    krt = kr.T
    for hh in range(MLA_HEADS):
        sl = slice(hh * LANES, (hh + 1) * LANES)
        q_ref[0, hh] = (qe[:, sl] * qmul).astype(BF16)
        k_ref[0, hh] = (ket[sl, :] + krt).astype(BF16)
        v_ref[0, hh] = jnp.where(lane == (MLA_V if hh % 2 == 0 else 0), 1.0, ve[:, sl]).astype(BF16)

    gates_ref[0] = _sigmoid(proj(_C_GATE, _C_CQ)).astype(BF16)


def _inproj(x, pos, g_mix, w_main, gbias, g_q, g_kv, wuq, wuk, wuv, invf, tm):
    B, S, D = x.shape
    grid = (B, S // tm)
    tok = lambda w: pl.BlockSpec((1, tm, w), lambda b, i: (b, i, 0))
    head = pl.BlockSpec((1, MLA_HEADS, tm, LANES), lambda b, i: (b, 0, i, 0))
    head_t = pl.BlockSpec((1, MLA_HEADS, LANES, tm), lambda b, i: (b, 0, 0, i))
    out_shape = (
        jax.ShapeDtypeStruct((B, S, N_BRANCH * D_MODEL), BF16),
        jax.ShapeDtypeStruct((B, MLA_HEADS, S, LANES), BF16),
        jax.ShapeDtypeStruct((B, MLA_HEADS, LANES, S), BF16),
        jax.ShapeDtypeStruct((B, MLA_HEADS, S, LANES), BF16),
        jax.ShapeDtypeStruct((B, S, ML_WIDTH), F32),
        jax.ShapeDtypeStruct((B, S, ML_WIDTH), BF16),
        jax.ShapeDtypeStruct((B, S, ML_WIDTH), BF16),
        jax.ShapeDtypeStruct((B, S, 2 * LANES), F32),
        jax.ShapeDtypeStruct((B, S, MEM_WIDTH), BF16),
    )
    out_specs = (tok(N_BRANCH * D_MODEL), head, head_t, head, tok(ML_WIDTH), tok(ML_WIDTH),
                 tok(ML_WIDTH), tok(2 * LANES), tok(MEM_WIDTH))
    consts = (g_mix, w_main, gbias, g_q, g_kv, wuq, wuk, wuv, invf)
    in_specs = [tok(D), pl.BlockSpec((1, tm, 1), lambda b, i: (b, i, 0))] + [_const_spec(a.shape) for a in consts]
    return pl.pallas_call(
        _inproj_kernel, grid=grid, in_specs=in_specs, out_specs=out_specs, out_shape=out_shape,
        compiler_params=pltpu.CompilerParams(dimension_semantics=("parallel", "parallel"),
                                             vmem_limit_bytes=VMEM_LIMIT),
        name="inproj",
    )(x, pos, *consts)


def _mla_kernel(q_ref, kt_ref, v_ref, o_ref, s_buf, p_buf):
    S = q_ref.shape[2]
    tu = s_buf.shape[1]
    low = lax.broadcasted_iota(jnp.int32, (tu, LANES), 1) < MLA_V
    unit = 0
    for pair in range(MLA_STEP_HEADS // 2):
        for t in range(S // tu):
            rows = slice(t * tu, (t + 1) * tu)
            outs = []
            for j in range(2):
                hh = 2 * pair + j
                slot = unit % 2
                unit += 1
                s_buf[slot] = _dot(q_ref[0, hh, rows, :], kt_ref[0, hh])
                s = s_buf[slot]
                p_buf[slot] = jnp.exp2(s - jnp.max(s, axis=-1, keepdims=True)).astype(BF16)
                r = _dot(p_buf[slot], v_ref[0, hh])
                den = r[:, MLA_V:MLA_V + 1] if j == 0 else r[:, 0:1]
                outs.append(r / den)
            o_ref[0, rows, pair * LANES:(pair + 1) * LANES] = jnp.where(low, outs[0], outs[1]).astype(BF16)


def _mla_attention(q, kt, v):
    B, H, S, _ = q.shape
    hs = MLA_STEP_HEADS
    tu = min(MLA_UNIT_ROWS, S)
    heads = lambda shape: pl.BlockSpec((1, hs) + shape, lambda b, p: (b, p, 0, 0))
    return pl.pallas_call(
        _mla_kernel, grid=(B, H // hs),
        in_specs=[heads((S, LANES)), heads((LANES, S)), heads((S, LANES))],
        out_specs=pl.BlockSpec((1, S, hs * MLA_V), lambda b, p: (b, 0, p)),
        out_shape=jax.ShapeDtypeStruct((B, S, H * MLA_V), BF16),
        scratch_shapes=[pltpu.VMEM((2, tu, S), F32),
                        pltpu.VMEM((2, tu, S), BF16)],
        compiler_params=pltpu.CompilerParams(dimension_semantics=("parallel", "parallel"),
                                             vmem_limit_bytes=VMEM_LIMIT),
        name="mla_attn",
    )(q, kt, v)


def _gate_lane(d, r, h):
    return ML_DIR_LANES * d + ML_HEADS * r + h


def _split3(x):
    hi = x.astype(BF16).astype(F32)
    r = x - hi
    mid = r.astype(BF16).astype(F32)
    lo = (r - mid).astype(BF16).astype(F32)
    return hi, mid, lo


def _pieces(srcs, rep):
    out = None
    for i, src in enumerate(srcs):
        for j, piece in enumerate(_split3(src)):
            out = piece if out is None else jnp.where(rep == 3 * i + j, piece, out)
    return out.astype(BF16)


def _cummax_dirs(x, fwd_lane):
    n = x.shape[0]
    row = lax.broadcasted_iota(jnp.int32, x.shape, 0)
    sh = 1
    while sh < n:
        up = jnp.where(row >= sh, pltpu.roll(x, sh, 0), -jnp.inf)
        dn = jnp.where(row < n - sh, pltpu.roll(x, n - sh, 0), -jnp.inf)
        x = jnp.maximum(x, jnp.where(fwd_lane, up, dn))
        sh *= 2
    return x


def _mlstm_selectors():
    sel_a = np.zeros((LANES, 2, ML_PAIRS, 4 * LANES), np.float32)
    sel_m = np.zeros((LANES, 2, ML_PAIRS, LANES), np.float32)
    sel_d = np.zeros((LANES, 2, ML_PAIRS, 4 * LANES), np.float32)
    for d in range(2):
        for p in range(ML_PAIRS):
            for j in range(2):
                h = 2 * p + j
                for r in range(3):
                    a = _gate_lane(d, r, h)
                    g = _gate_lane(d, r + 3, h)
                    sel_a[a, d, p, j * LANES:(j + 1) * LANES] = 1
                    sel_a[a, d, p, 2 * LANES + j * ML_DK:2 * LANES + (j + 1) * ML_DK] = 1
                    sel_a[g, d, p, 3 * LANES + j * ML_DV:3 * LANES + (j + 1) * ML_DV] = 1
                    sel_m[a, d, p, j * ML_DK:(j + 1) * ML_DK] = 1
                    sel_d[a, d, p, j * 2 * LANES:(j + 1) * 2 * LANES] = 1
    as_bf = lambda m: jnp.asarray(m.reshape(LANES, -1), BF16)
    return as_bf(sel_a), as_bf(sel_m), as_bf(sel_d)


def _mlstm_kernel(u_ref, v_ref, so_ref, g_ref, cw_ref, wq_ref, wkt_ref, ghead_ref, sela_ref, selm_ref,
                  seld_ref, o_ref,
                  upad, h_s, q_s, kt_s, gate_s, a_s, ct_s, wst_s, rows_s, mpb_s, dec_s, c_s):
    S = u_ref.shape[1]
    L = ML_CHUNK
    nc = S // L
    half = CONV_WIDTH // 2

    zpad = jnp.zeros((CONV_PAD, ML_WIDTH), F32)
    upad[0:CONV_PAD, :] = zpad
    upad[CONV_PAD + S:CONV_PAD + S + CONV_PAD, :] = zpad
    upad[CONV_PAD:CONV_PAD + S, :] = u_ref[0]
    c_s[...] = jnp.zeros(c_s.shape, F32)
    rows_s[...] = jnp.zeros(rows_s.shape, F32)

    def conv_chunk(r):
        acc = jnp.zeros((L, ML_WIDTH), F32)
        for w in range(CONV_WIDTH):
            start = CONV_PAD - half + w + r * L
            acc = acc + upad[start:start + L, :] * cw_ref[w:w + 1, :]
        c = (acc * _sigmoid(acc)).astype(BF16)
        q_s[r * L:(r + 1) * L, :] = _dot(c, wq_ref[...]).astype(BF16)
        kt_s[r] = _dot_nt(wkt_ref[...], c).astype(BF16)

    def gate_lanes(nrows):
        lane = lax.broadcasted_iota(jnp.int32, (nrows, LANES), 1)
        fwd = lane < ML_DIR_LANES
        return fwd, jnp.right_shift(jnp.where(fwd, lane, lane - ML_DIR_LANES), 3)

    fwd_lane, rep = gate_lanes(L)
    fwd_row, _ = gate_lanes(1)
    tri_l = jnp.where(lax.broadcasted_iota(jnp.int32, (L, L), 0) >= lax.broadcasted_iota(jnp.int32, (L, L), 1),
                      1.0, 0.0).astype(BF16)
    btot, gmax = [], []
    for c in range(nc):
        rs = slice(c * L, (c + 1) * L)
        gf = g_ref[0, rs, LANES:2 * LANES]
        lf = jnp.minimum(gf, 0.0) - jnp.log1p(jnp.exp(-jnp.abs(gf)))
        bf = sum(_dot(tri_l, piece.astype(BF16)) for piece in _split3(lf))
        bt = bf[L - 1:L, :]
        bcum = jnp.where(fwd_lane, bf, bt - bf + lf)
        cc = g_ref[0, rs, 0:LANES] - bcum
        cmx = _cummax_dirs(cc, fwd_lane)
        gate_s[0, rs, :] = bcum
        gate_s[1, rs, :] = cc
        gate_s[2, rs, :] = cmx
        btot.append(bt)
        gmax.append(jnp.where(fwd_row, cmx[L - 1:L, :], cmx[0:1, :]))

    def scan_m(order):
        prev, new = [None] * nc, [None] * nc
        m = jnp.full((1, LANES), NEG_INIT, F32)
        for c in order:
            prev[c] = m
            m = jnp.maximum(btot[c] + m, btot[c] + gmax[c])
            new[c] = m
        return prev, new

    prev_f, new_f = scan_m(range(nc))
    prev_b, new_b = scan_m(reversed(range(nc)))

    for c in range(nc):
        rs = slice(c * L, (c + 1) * L)
        mprev = jnp.where(fwd_row, prev_f[c], prev_b[c])
        mnew = jnp.where(fwd_row, new_f[c], new_b[c])
        bcum, cc, cmx = gate_s[0, rs, :], gate_s[1, rs, :], gate_s[2, rs, :]
        mstab = jnp.maximum(mprev, cmx)
        a_s[rs, :] = _pieces((-mstab, -(bcum + mstab)), rep)
        ct_s[c] = cc.T
        wst_s[c] = jnp.exp(btot[c] + cc - mnew).T
        rows_s[0, c:c + 1, :] = mprev
        rows_s[1, c:c + 1, :] = jnp.exp(btot[c] + mprev - mnew)
    _, rep_c = gate_lanes(rows_s.shape[1])
    mpb = _dot(_pieces((rows_s[0],), rep_c), selm_ref[...])
    dec = _dot(_pieces((rows_s[1],), rep_c), seld_ref[...])
    for c in range(nc):
        mpb_s[c] = mpb[c:c + 1, :]
        dec_s[c] = dec[c:c + 1, :]

    t_i = lax.broadcasted_iota(jnp.int32, (L, L), 0)
    s_i = lax.broadcasted_iota(jnp.int32, (L, L), 1)
    low = lax.broadcasted_iota(jnp.int32, (L, LANES), 1) < ML_DV
    ones_e = jnp.where(low, 1.0, 0.0).astype(BF16)
    ones_o = jnp.where(low, 0.0, 1.0).astype(BF16)
    blk = 4 * LANES

    def finish(ci, p, hp):
        rows = slice(ci * L, (ci + 1) * L)
        sl = slice(p * LANES, (p + 1) * LANES)
        sq = hp * hp
        s_e = jnp.sum(jnp.where(low, sq, 0.0), axis=-1, keepdims=True)
        s_o = jnp.sum(jnp.where(low, 0.0, sq), axis=-1, keepdims=True)
        ms = jnp.where(low, s_e, s_o) * (1.0 / ML_DV)
        out = hp * lax.rsqrt(ms + EPS) * ghead_ref[:, sl] * so_ref[0, rows, sl].astype(F32)
        o_ref[0, rows, sl] = out.astype(BF16)

    def chunk_dir(ci, d, first):
        rows = slice(ci * L, (ci + 1) * L)
        allowed = t_i >= s_i if d == 0 else t_i <= s_i
        z = _dot(a_s[rows, :], sela_ref[:, d * ML_PAIRS * blk:(d + 1) * ML_PAIRS * blk])
        for p in range(ML_PAIRS):
            le = _gate_lane(d, 0, 2 * p)
            lo = le + 1
            dp = d * ML_PAIRS + p
            sl = slice(p * LANES, (p + 1) * LANES)
            zb = z[:, p * blk:(p + 1) * blk]
            qp = q_s[rows, sl]
            ktp = kt_s[ci, sl, :]
            vp = v_ref[0, rows, sl]
            zk = jnp.zeros((ML_DK, L), BF16)
            yt = jnp.concatenate([jnp.concatenate([ktp[:ML_DK], zk], axis=0),
                                  jnp.concatenate([zk, ktp[ML_DK:]], axis=0)], axis=1)
            qk = _dot(qp, yt)
            e_e = jnp.where(allowed, zb[:, 0:LANES] + ct_s[ci, le:le + 1, :], -jnp.inf)
            e_o = jnp.where(allowed, zb[:, LANES:2 * LANES] + ct_s[ci, lo:lo + 1, :], -jnp.inf)
            e_q = zb[:, 2 * LANES:3 * LANES] + mpb_s[ci, :, dp * LANES:(dp + 1) * LANES]
            pw = jnp.exp(jnp.concatenate([e_e, e_o, e_q], axis=1))
            x = (pw * jnp.concatenate([qk, qp.astype(F32)], axis=1)).astype(BF16)
            zero = jnp.zeros_like(vp)
            v2 = jnp.concatenate([jnp.concatenate([jnp.where(low, vp, zero), ones_e], axis=1),
                                  jnp.concatenate([jnp.where(low, zero, vp), ones_o], axis=1)], axis=0)
            cst = c_s[d, p]
            r = _dot(x, jnp.concatenate([v2, cst.astype(BF16)], axis=0))
            clamp = jnp.exp(zb[:, 3 * LANES:4 * LANES])
            hdir = r[:, 0:LANES] / jnp.maximum(jnp.abs(r[:, LANES:2 * LANES]), clamp)
            if first:
                h_s[rows, sl] = hdir
            else:
                finish(ci, p, h_s[rows, sl] + hdir)
            wsrow = jnp.concatenate([wst_s[ci, le:le + 1, :], wst_s[ci, lo:lo + 1, :]], axis=1)
            upd = _dot((yt.astype(F32) * wsrow).astype(BF16), v2)
            drow = dec_s[ci, :, dp * blk:(dp + 1) * blk]
            dmat = jnp.concatenate([jnp.broadcast_to(drow[:, 0:2 * LANES], (ML_DK, 2 * LANES)),
                                    jnp.broadcast_to(drow[:, 2 * LANES:4 * LANES], (ML_DK, 2 * LANES))], axis=0)
            c_s[d, p] = cst * dmat + upd

    conv_chunk(0)
    conv_chunk(nc - 1)
    for j in range(nc):
        first = j < nc - 1 - j
        chunk_dir(j, 0, first)
        chunk_dir(nc - 1 - j, 1, first)
        if j + 1 < nc - 2 - j:
            conv_chunk(j + 1)
            conv_chunk(nc - 2 - j)


def _mlstm(u, v, so, g, conv_w, wq_bd, wkt_bd, g_head):
    B, S, W = u.shape
    L = ML_CHUNK
    nc = S // L
    sels = _mlstm_selectors()
    seq = lambda w: pl.BlockSpec((1, S, w), lambda b: (b, 0, 0))
    consts = (conv_w, wq_bd, wkt_bd, g_head) + sels
    return pl.pallas_call(
        _mlstm_kernel, grid=(B,),
        in_specs=[seq(W), seq(W), seq(W), seq(2 * LANES)] + [_const_spec(a.shape) for a in consts],
        out_specs=seq(W),
        out_shape=jax.ShapeDtypeStruct((B, S, W), BF16),
        scratch_shapes=[
            pltpu.VMEM((S + 2 * CONV_PAD, W), F32),
            pltpu.VMEM((S, W), F32),
            pltpu.VMEM((S, W), BF16),
            pltpu.VMEM((nc, W, L), BF16),
            pltpu.VMEM((3, S, LANES), F32),
            pltpu.VMEM((S, LANES), BF16),
            pltpu.VMEM((nc, LANES, L), F32),
            pltpu.VMEM((nc, LANES, L), F32),
            pltpu.VMEM((2, -(-nc // 16) * 16, LANES), F32),
            pltpu.VMEM((nc, 1, 2 * ML_PAIRS * LANES), F32),
            pltpu.VMEM((nc, 1, 2 * ML_PAIRS * 4 * LANES), F32),
            pltpu.VMEM((2, ML_PAIRS, LANES, 2 * LANES), F32),
        ],
        compiler_params=pltpu.CompilerParams(dimension_semantics=("parallel",),
                                             vmem_limit_bytes=VMEM_LIMIT),
        name="mlstm",
    )(u, v, so, g, *consts)


def _memkv_kernel(mem_ref, g_ref, w_ref, k_ref, v_ref):
    mn = _rms(mem_ref[0], g_ref[...]).astype(BF16)
    kv = _dot(mn, w_ref[...])
    k_ref[0] = kv[:, :MEM_WIDTH].astype(BF16)
    v_ref[0] = kv[:, MEM_WIDTH:].astype(BF16)


def _memkv(mem, mem_g, w_kv):
    B, M, D = mem.shape
    blk = lambda w: pl.BlockSpec((1, M, w), lambda b: (b, 0, 0))
    return pl.pallas_call(
        _memkv_kernel, grid=(B,),
        in_specs=[blk(D), _const_spec(mem_g.shape), _const_spec(w_kv.shape)],
        out_specs=(blk(MEM_WIDTH), blk(MEM_WIDTH)),
        out_shape=(jax.ShapeDtypeStruct((B, M, MEM_WIDTH), BF16),) * 2,
        compiler_params=pltpu.CompilerParams(dimension_semantics=("parallel",)),
        name="memkv",
    )(mem, mem_g, w_kv)


def _memattn_kernel(q_ref, k_ref, v_ref, o_ref):
    scale = MEM_HEAD_DIM ** -0.5
    for hh in range(MEM_HEADS):
        sl = slice(hh * MEM_HEAD_DIM, (hh + 1) * MEM_HEAD_DIM)
        s = _dot_nt(q_ref[0, :, sl], k_ref[0, :, sl]) * scale
        e = jnp.exp(s - jnp.max(s, axis=-1, keepdims=True))
        den = jnp.sum(e, axis=-1, keepdims=True)
        o_ref[0, :, sl] = (_dot(e.astype(BF16), v_ref[0, :, sl]) / den).astype(BF16)


def _memattn(q, k, v, tq):
    B, S, W = q.shape
    M = k.shape[1]
    return pl.pallas_call(
        _memattn_kernel, grid=(B, S // tq),
        in_specs=[pl.BlockSpec((1, tq, W), lambda b, i: (b, i, 0)),
                  pl.BlockSpec((1, M, W), lambda b, i: (b, 0, 0)),
                  pl.BlockSpec((1, M, W), lambda b, i: (b, 0, 0))],
        out_specs=pl.BlockSpec((1, tq, W), lambda b, i: (b, i, 0)),
        out_shape=jax.ShapeDtypeStruct((B, S, W), BF16),
        compiler_params=pltpu.CompilerParams(dimension_semantics=("parallel", "parallel")),
        name="memattn",
    )(q, k, v)


def _out_kernel(x_ref, gates_ref, omla_ref, oml_ref, omem_ref, wb_ref, wout_ref, gffn_ref,
                wg_ref, wu_ref, wd_ref, gfin_ref, o_ref):
    merged = None
    for b, oref in enumerate((omla_ref, oml_ref, omem_ref)):
        y = _dot(oref[...], wb_ref[b]) * gates_ref[:, b * D_MODEL:(b + 1) * D_MODEL].astype(F32)
        merged = y if merged is None else merged + y
    x1 = x_ref[...] + _dot(merged.astype(BF16), wout_ref[...])
    h2 = _rms(x1, gffn_ref[...]).astype(BF16)
    hg = _dot(h2, wg_ref[...])
    a = (hg * _sigmoid(hg) * _dot(h2, wu_ref[...])).astype(BF16)
    x2 = x1 + _dot(a, wd_ref[...])
    o_ref[...] = _rms(x2, gfin_ref[...])


def _out_block(x, gates, o_mla, o_ml, o_mem, w_branch, w_out, g_ffn, w_g, w_u, w_d, g_final, tm):
    T, D = x.shape
    tok = lambda w: pl.BlockSpec((tm, w), lambda i: (i, 0))
    consts = (w_branch, w_out, g_ffn, w_g, w_u, w_d, g_final)
    return pl.pallas_call(
        _out_kernel, grid=(T // tm,),
        in_specs=[tok(D), tok(N_BRANCH * D), tok(o_mla.shape[1]), tok(o_ml.shape[1]), tok(o_mem.shape[1])]
        + [_const_spec(a.shape) for a in consts],
        out_specs=tok(D),
        out_shape=jax.ShapeDtypeStruct((T, D), x.dtype),
        compiler_params=pltpu.CompilerParams(dimension_semantics=("parallel",),
                                             vmem_limit_bytes=VMEM_LIMIT),
        name="out_block",
    )(x, gates, o_mla, o_ml, o_mem, *consts)


def _prep_in_weights(w_in, gate_bias):
    sizes = (N_BRANCH * D_MODEL, MLA_Q_RANK, MLA_KV_RANK, MLA_ROPE, ML_WIDTH, ML_WIDTH, ML_WIDTH,
             N_ML_GATES, MEM_WIDTH)
    offs = np.cumsum((0,) + sizes)
    w_gate, w_cq, w_ckv, w_kr, w_mu, w_mv, w_mo, w_mg, w_memq = [
        w_in[:, offs[i]:offs[i + 1]] for i in range(len(sizes))]
    half = MLA_ROPE // 2
    y1, y2 = w_kr[:, :half], w_kr[:, half:]
    z = lambda n: jnp.zeros((D_MODEL, n), w_in.dtype)
    w_kr = jnp.concatenate([z(MLA_NOPE), y1, y2, -y2, y1], axis=1)
    H = ML_HEADS
    pad = LANES - 2 * ML_DIR_LANES

    def gate_block(cols, fwd, bwd, zeros):
        return jnp.concatenate([cols(fwd)] * ML_REPL + [cols(bwd)] * ML_REPL + [zeros], axis=-1)

    wcols = lambda g: w_mg[:, g * H:(g + 1) * H]
    w_gi = gate_block(wcols, 0, 2, z(pad))
    w_gf = gate_block(wcols, 1, 3, z(pad))
    w_main = jnp.concatenate([w_gate, w_cq, w_ckv, w_kr, w_mu, w_mv, w_mo, w_memq, w_gi, w_gf],
                             axis=1).astype(BF16)
    bcols = lambda g: gate_bias[g]
    zb = jnp.zeros((pad,), gate_bias.dtype)
    gbias = jnp.concatenate([gate_block(bcols, 0, 2, zb), gate_block(bcols, 1, 3, zb)])[None, :]
    return w_main, gbias.astype(F32)


def _prep_mla_weights(w_uq, w_uk, w_uv):
    half = MLA_ROPE // 2
    qn = w_uq[:, :, :MLA_NOPE]
    x1 = w_uq[:, :, MLA_NOPE:MLA_NOPE + half]
    x2 = w_uq[:, :, MLA_NOPE + half:]
    wuq = jnp.concatenate([qn, x1, x2, -x2, x1], axis=-1).reshape(MLA_Q_RANK, MLA_HEADS * LANES)
    wuk = jnp.concatenate([w_uk, jnp.zeros_like(w_uk)], axis=-1).reshape(MLA_KV_RANK, MLA_HEADS * LANES).T
    wv = w_uv.reshape(MLA_KV_RANK, MLA_HEADS // 2, 2, MLA_V)
    zv = jnp.zeros_like(wv[:, :, 0])
    wuv = jnp.concatenate([wv[:, :, 0], zv, zv, wv[:, :, 1]], axis=-1).reshape(MLA_KV_RANK, MLA_HEADS * LANES)
    return wuq.astype(BF16), wuk.astype(BF16), wuv.astype(BF16)


def _block_diag(w):
    H, d, _ = w.shape
    eye = jnp.eye(H, dtype=w.dtype)
    return (eye[:, None, :, None] * w[:, :, None, :]).reshape(H * d, H * d)


def _layer(x, mem, positions, g_mix, w_in, mla_g_q, mla_g_kv, mla_w_uq, mla_w_uk, mla_w_uv,
           ml_conv_w, ml_w_q, ml_w_k, ml_gate_bias, ml_g_head, mem_g, mem_w_kv,
           w_branch, w_out, g_ffn, w_ffn_gate, w_ffn_up, w_ffn_down, g_final):
    B, S, D = x.shape
    row = lambda g: g.reshape(1, -1).astype(F32)
    w_main, gbias = _prep_in_weights(w_in, ml_gate_bias)
    wuq, wuk, wuv = _prep_mla_weights(mla_w_uq, mla_w_uk, mla_w_uv)
    half = MLA_ROPE // 2
    inv = ROPE_THETA ** (-jnp.arange(half, dtype=F32) / half)
    phase = np.where(np.arange(LANES) % MLA_NOPE < MLA_ROPE, 0.0, 0.5 * np.pi).astype(np.float32)
    invf = jnp.stack([jnp.tile(inv, LANES // half), jnp.asarray(phase)])

    gates, q, k, v, mu, mv, mo, mg, memq = _inproj(
        x, positions.reshape(B, S, 1), row(g_mix), w_main, gbias, row(mla_g_q), row(mla_g_kv),
        wuq, wuk, wuv, invf, tm=min(512, S))
    o_mla = _mla_attention(q, k, v)
    o_ml = _mlstm(mu, mv, mo, mg, ml_conv_w.astype(F32), _block_diag(ml_w_q).astype(BF16),
                  _block_diag(ml_w_k * (ML_DK ** -0.5)).T.astype(BF16), row(ml_g_head))
    mk, mvv = _memkv(mem, row(mem_g), mem_w_kv.astype(BF16))
    o_mem = _memattn(memq, mk, mvv, tq=min(512, S))

    T = B * S
    flat = lambda a: a.reshape(T, a.shape[-1])
    out = _out_block(flat(x), flat(gates), flat(o_mla), flat(o_ml), flat(o_mem),
                     w_branch.astype(BF16), w_out.astype(BF16), row(g_ffn), w_ffn_gate.astype(BF16),
                     w_ffn_up.astype(BF16), w_ffn_down.astype(BF16), row(g_final), tm=min(512, T))
    return out.reshape(B, S, D)


def kernel(x, mem, positions, g_mix, w_in, mla_g_q, mla_g_kv, mla_w_uq, mla_w_uk, mla_w_uv, ml_conv_w,
           ml_w_q, ml_w_k, ml_gate_bias, ml_g_head, mem_g, mem_w_kv, w_branch, w_out, g_ffn,
           w_ffn_gate, w_ffn_up, w_ffn_down, g_final):
    depth = g_mix.shape[0]
    assert depth == 1, "the final norm is fused into the single layer's output kernel"
    return _layer(x, mem, positions, g_mix[0], w_in[0], mla_g_q[0], mla_g_kv[0], mla_w_uq[0],
                  mla_w_uk[0], mla_w_uv[0], ml_conv_w[0], ml_w_q[0], ml_w_k[0], ml_gate_bias[0],
                  ml_g_head[0], mem_g[0], mem_w_kv[0], w_branch[0], w_out[0], g_ffn[0],
                  w_ffn_gate[0], w_ffn_up[0], w_ffn_down[0], g_final)
```

```python
import math

import numpy as np
import jax
import jax.numpy as jnp
from jax import lax
from jax.experimental import pallas as pl
from jax.experimental.pallas import tpu as pltpu

D_MODEL = 1024
MEM_LEN = 256
EPS = 1e-6
MLA_HEADS = 8
MLA_NOPE = 64
MLA_ROPE = 32
MLA_V = 64
MLA_Q_RANK = 384
MLA_KV_RANK = 256
ROPE_THETA = 10000.0
ML_HEADS = 8
ML_DK = 64
ML_DV = 64
ML_WIDTH = ML_HEADS * ML_DV
CONV_WIDTH = 5
MEM_HEADS = 4
MEM_HEAD_DIM = 128
MEM_WIDTH = MEM_HEADS * MEM_HEAD_DIM
N_BRANCH = 3
N_ML_GATES = 4 * ML_HEADS
D_FF = 2816

LANES = 128
MLA_UNIT_ROWS = 512
MLA_STEP_HEADS = 8
ML_CHUNK = LANES
ML_PAIRS = ML_HEADS // 2
ML_REPL = 6
ML_DIR_LANES = ML_REPL * ML_HEADS
NEG_INIT = -1e30
CONV_PAD = 8
VMEM_LIMIT = 56 * 1024 * 1024

BF16 = jnp.bfloat16
F32 = jnp.float32

_C_GATE = 0
_C_CQ = _C_GATE + N_BRANCH * D_MODEL
_C_CKV = _C_CQ + MLA_Q_RANK
_C_KR = _C_CKV + MLA_KV_RANK
_C_MU = _C_KR + LANES
_C_MV = _C_MU + ML_WIDTH
_C_MO = _C_MV + ML_WIDTH
_C_MEMQ = _C_MO + ML_WIDTH
_C_MGI = _C_MEMQ + MEM_WIDTH
_C_MGF = _C_MGI + LANES
_C_END = _C_MGF + LANES


def _const_spec(shape):
    nd = len(shape)
    return pl.BlockSpec(shape, lambda *_: (0,) * nd, pipeline_mode=pl.Buffered(1))


def _sigmoid(x):
    return 1.0 / (1.0 + jnp.exp(-x))


def _rms(x, g):
    return x * lax.rsqrt(jnp.mean(x * x, axis=-1, keepdims=True) + EPS) * g


def _dot(a, b):
    return jnp.dot(a, b, preferred_element_type=F32)


def _dot_nt(a, b):
    return lax.dot_general(a, b, (((1,), (1,)), ((), ())), preferred_element_type=F32)


def _inproj_kernel(x_ref, pos_ref, gmix_ref, w_ref, gbias_ref, gq_ref, gkv_ref, wuq_ref, wuk_ref,
                   wuv_ref, rope_ref,
                   gates_ref, q_ref, k_ref, v_ref, mu_ref, mv_ref, mo_ref, mg_ref, memq_ref):
    x = x_ref[0]
    tm = x.shape[0]
    h = _rms(x, gmix_ref[...]).astype(BF16)

    def proj(a, b):
        return _dot(h, w_ref[:, a:b])

    cqn = _rms(proj(_C_CQ, _C_CKV), gq_ref[...]).astype(BF16)
    ckvn = _rms(proj(_C_CKV, _C_KR), gkv_ref[...]).astype(BF16)
    qe = _dot(cqn, wuq_ref[...])
    ket = _dot_nt(wuk_ref[...], ckvn)
    ve = _dot(ckvn, wuv_ref[...])

    mu_ref[0] = proj(_C_MU, _C_MV)
    mv_ref[0] = proj(_C_MV, _C_MO).astype(BF16)
    mo_ref[0] = _sigmoid(proj(_C_MO, _C_MEMQ)).astype(BF16)
    memq_ref[0] = (proj(_C_MEMQ, _C_MGI) * (MEM_HEAD_DIM ** -0.5 * math.log2(math.e))).astype(BF16)
    mg_ref[0] = proj(_C_MGI, _C_END) + gbias_ref[...]

    lane = lax.broadcasted_iota(jnp.int32, (tm, LANES), 1)
    hm = tm // 2
    pos = pos_ref[0].astype(F32)
    pos2 = jnp.where(lax.broadcasted_iota(jnp.int32, (hm, LANES), 1) < MLA_NOPE, pos[:hm], pos[hm:])
    rope2 = jnp.cos(pos2 * rope_ref[0:1, :] - rope_ref[1:2, :])
    rope = jnp.concatenate([pltpu.roll(rope2, MLA_NOPE, 1), rope2], axis=0)
    kra = proj(_C_KR, _C_MU) * rope
    kr = jnp.where(lane < MLA_NOPE, 0.0,
                   kra + pltpu.roll(kra, MLA_ROPE, 1) + pltpu.roll(kra, LANES - MLA_ROPE, 1))
    q_scale = (MLA_NOPE + MLA_ROPE) ** -0.5 * math.log2(math.e)
    qmul = jnp.where(lane < MLA_NOPE, 1.0, rope) * q_scale
    krt = kr.T
    for hh in range(MLA_HEADS):
        sl = slice(hh * LANES, (hh + 1) * LANES)
        q_ref[0, hh] = (qe[:, sl] * qmul).astype(BF16)
        k_ref[0, hh] = (ket[sl, :] + krt).astype(BF16)
        v_ref[0, hh] = jnp.where(lane == (MLA_V if hh % 2 == 0 else 0), 1.0, ve[:, sl]).astype(BF16)

    gates_ref[0] = _sigmoid(proj(_C_GATE, _C_CQ)).astype(BF16)


def _inproj(x, pos, g_mix, w_main, gbias, g_q, g_kv, wuq, wuk, wuv, invf, tm):
    B, S, D = x.shape
    grid = (B, S // tm)
    tok = lambda w: pl.BlockSpec((1, tm, w), lambda b, i: (b, i, 0))
    head = pl.BlockSpec((1, MLA_HEADS, tm, LANES), lambda b, i: (b, 0, i, 0))
    head_t = pl.BlockSpec((1, MLA_HEADS, LANES, tm), lambda b, i: (b, 0, 0, i))
    out_shape = (
        jax.ShapeDtypeStruct((B, S, N_BRANCH * D_MODEL), BF16),
        jax.ShapeDtypeStruct((B, MLA_HEADS, S, LANES), BF16),
        jax.ShapeDtypeStruct((B, MLA_HEADS, LANES, S), BF16),
        jax.ShapeDtypeStruct((B, MLA_HEADS, S, LANES), BF16),
        jax.ShapeDtypeStruct((B, S, ML_WIDTH), F32),
        jax.ShapeDtypeStruct((B, S, ML_WIDTH), BF16),
        jax.ShapeDtypeStruct((B, S, ML_WIDTH), BF16),
        jax.ShapeDtypeStruct((B, S, 2 * LANES), F32),
        jax.ShapeDtypeStruct((B, S, MEM_WIDTH), BF16),
    )
    out_specs = (tok(N_BRANCH * D_MODEL), head, head_t, head, tok(ML_WIDTH), tok(ML_WIDTH),
                 tok(ML_WIDTH), tok(2 * LANES), tok(MEM_WIDTH))
    consts = (g_mix, w_main, gbias, g_q, g_kv, wuq, wuk, wuv, invf)
    in_specs = [tok(D), pl.BlockSpec((1, tm, 1), lambda b, i: (b, i, 0))] + [_const_spec(a.shape) for a in consts]
    return pl.pallas_call(
        _inproj_kernel, grid=grid, in_specs=in_specs, out_specs=out_specs, out_shape=out_shape,
        compiler_params=pltpu.CompilerParams(dimension_semantics=("parallel", "parallel"),
                                             vmem_limit_bytes=VMEM_LIMIT),
        name="inproj",
    )(x, pos, *consts)


def _mla_kernel(q_ref, kt_ref, v_ref, o_ref, s_buf, p_buf):
    S = q_ref.shape[2]
    tu = s_buf.shape[1]
    low = lax.broadcasted_iota(jnp.int32, (tu, LANES), 1) < MLA_V
    unit = 0
    for pair in range(MLA_STEP_HEADS // 2):
        for t in range(S // tu):
            rows = slice(t * tu, (t + 1) * tu)
            outs = []
            for j in range(2):
                hh = 2 * pair + j
                slot = unit % 2
                unit += 1
                s_buf[slot] = _dot(q_ref[0, hh, rows, :], kt_ref[0, hh])
                s = s_buf[slot]
                p_buf[slot] = jnp.exp2(s - jnp.max(s, axis=-1, keepdims=True)).astype(BF16)
                r = _dot(p_buf[slot], v_ref[0, hh])
                den = r[:, MLA_V:MLA_V + 1] if j == 0 else r[:, 0:1]
                outs.append(r / den)
            o_ref[0, rows, pair * LANES:(pair + 1) * LANES] = jnp.where(low, outs[0], outs[1]).astype(BF16)


def _mla_attention(q, kt, v):
    B, H, S, _ = q.shape
    hs = MLA_STEP_HEADS
    tu = min(MLA_UNIT_ROWS, S)
    heads = lambda shape: pl.BlockSpec((1, hs) + shape, lambda b, p: (b, p, 0, 0))
    return pl.pallas_call(
        _mla_kernel, grid=(B, H // hs),
        in_specs=[heads((S, LANES)), heads((LANES, S)), heads((S, LANES))],
        out_specs=pl.BlockSpec((1, S, hs * MLA_V), lambda b, p: (b, 0, p)),
        out_shape=jax.ShapeDtypeStruct((B, S, H * MLA_V), BF16),
        scratch_shapes=[pltpu.VMEM((2, tu, S), F32),
                        pltpu.VMEM((2, tu, S), BF16)],
        compiler_params=pltpu.CompilerParams(dimension_semantics=("parallel", "parallel"),
                                             vmem_limit_bytes=VMEM_LIMIT),
        name="mla_attn",
    )(q, kt, v)


def _gate_lane(d, r, h):
    return ML_DIR_LANES * d + ML_HEADS * r + h


def _split3(x):
    hi = x.astype(BF16).astype(F32)
    r = x - hi
    mid = r.astype(BF16).astype(F32)
    lo = (r - mid).astype(BF16).astype(F32)
    return hi, mid, lo


def _pieces(srcs, rep):
    out = None
    for i, src in enumerate(srcs):
        for j, piece in enumerate(_split3(src)):
            out = piece if out is None else jnp.where(rep == 3 * i + j, piece, out)
    return out.astype(BF16)


def _cummax_dirs(x, fwd_lane):
    n = x.shape[0]
    row = lax.broadcasted_iota(jnp.int32, x.shape, 0)
    sh = 1
    while sh < n:
        if sh % 8 == 0:
            pad = jnp.full((sh, x.shape[1]), -jnp.inf, x.dtype)
            up = jnp.concatenate([pad, x[:n - sh]], axis=0)
            dn = jnp.concatenate([x[sh:], pad], axis=0)
        else:
            up = jnp.where(row >= sh, pltpu.roll(x, sh, 0), -jnp.inf)
            dn = jnp.where(row < n - sh, pltpu.roll(x, n - sh, 0), -jnp.inf)
        x = jnp.maximum(x, jnp.where(fwd_lane, up, dn))
        sh *= 2
    return x


def _mlstm_selectors():
    sel_a = np.zeros((LANES, 2, ML_PAIRS, 4 * LANES), np.float32)
    sel_m = np.zeros((LANES, 2, ML_PAIRS, LANES), np.float32)
    sel_d = np.zeros((LANES, 2, ML_PAIRS, 4 * LANES), np.float32)
    for d in range(2):
        for p in range(ML_PAIRS):
            for j in range(2):
                h = 2 * p + j
                for r in range(3):
                    a = _gate_lane(d, r, h)
                    g = _gate_lane(d, r + 3, h)
                    sel_a[a, d, p, j * LANES:(j + 1) * LANES] = 1
                    sel_a[a, d, p, 2 * LANES + j * ML_DK:2 * LANES + (j + 1) * ML_DK] = 1
                    sel_a[g, d, p, 3 * LANES + j * ML_DV:3 * LANES + (j + 1) * ML_DV] = 1
                    sel_m[a, d, p, j * ML_DK:(j + 1) * ML_DK] = 1
                    sel_d[a, d, p, j * 2 * LANES:(j + 1) * 2 * LANES] = 1
    as_bf = lambda m: jnp.asarray(m.reshape(LANES, -1), BF16)
    return as_bf(sel_a), as_bf(sel_m), as_bf(sel_d)


def _mlstm_kernel(u_ref, v_ref, so_ref, g_ref, cw_ref, wq_ref, wkt_ref, ghead_ref, sela_ref, selm_ref,
                  seld_ref, o_ref,
                  upad, h_s, q_s, kt_s, gate_s, a_s, ct_s, wst_s, rows_s, mpb_s, dec_s, c_s):
    S = u_ref.shape[1]
    L = ML_CHUNK
    nc = S // L
    half = CONV_WIDTH // 2

    zpad = jnp.zeros((CONV_PAD, ML_WIDTH), F32)
    upad[0:CONV_PAD, :] = zpad
    upad[CONV_PAD + S:CONV_PAD + S + CONV_PAD, :] = zpad
    upad[CONV_PAD:CONV_PAD + S, :] = u_ref[0]
    c_s[...] = jnp.zeros(c_s.shape, F32)
    rows_s[...] = jnp.zeros(rows_s.shape, F32)

    def conv_chunk(r):
        acc = jnp.zeros((L, ML_WIDTH), F32)
        for w in range(CONV_WIDTH):
            start = CONV_PAD - half + w + r * L
            acc = acc + upad[start:start + L, :] * cw_ref[w:w + 1, :]
        c = (acc * _sigmoid(acc)).astype(BF16)
        q_s[r * L:(r + 1) * L, :] = _dot(c, wq_ref[...]).astype(BF16)
        kt_s[r] = _dot_nt(wkt_ref[...], c).astype(BF16)

    def gate_lanes(nrows):
        lane = lax.broadcasted_iota(jnp.int32, (nrows, LANES), 1)
        fwd = lane < ML_DIR_LANES
        return fwd, jnp.right_shift(jnp.where(fwd, lane, lane - ML_DIR_LANES), 3)

    fwd_lane, rep = gate_lanes(L)
    fwd_row, _ = gate_lanes(1)
    tri_l = jnp.where(lax.broadcasted_iota(jnp.int32, (L, L), 0) >= lax.broadcasted_iota(jnp.int32, (L, L), 1),
                      1.0, 0.0).astype(BF16)
    btot, gmax = [], []
    for c in range(nc):
        rs = slice(c * L, (c + 1) * L)
        gf = g_ref[0, rs, LANES:2 * LANES]
        lf = jnp.minimum(gf, 0.0) - jnp.log1p(jnp.exp(-jnp.abs(gf)))
        bf = sum(_dot(tri_l, piece.astype(BF16)) for piece in _split3(lf))
        bt = bf[L - 1:L, :]
        bcum = jnp.where(fwd_lane, bf, bt - bf + lf)
        cc = g_ref[0, rs, 0:LANES] - bcum
        cmx = _cummax_dirs(cc, fwd_lane)
        gate_s[0, rs, :] = bcum
        gate_s[1, rs, :] = cc
        gate_s[2, rs, :] = cmx
        btot.append(bt)
        gmax.append(jnp.where(fwd_row, cmx[L - 1:L, :], cmx[0:1, :]))

    def scan_m(order):
        prev, new = [None] * nc, [None] * nc
        m = jnp.full((1, LANES), NEG_INIT, F32)
        for c in order:
            prev[c] = m
            m = jnp.maximum(btot[c] + m, btot[c] + gmax[c])
            new[c] = m
        return prev, new

    prev_f, new_f = scan_m(range(nc))
    prev_b, new_b = scan_m(reversed(range(nc)))

    for c in range(nc):
        rs = slice(c * L, (c + 1) * L)
        mprev = jnp.where(fwd_row, prev_f[c], prev_b[c])
        mnew = jnp.where(fwd_row, new_f[c], new_b[c])
        bcum, cc, cmx = gate_s[0, rs, :], gate_s[1, rs, :], gate_s[2, rs, :]
        mstab = jnp.maximum(mprev, cmx)
        a_s[rs, :] = _pieces((-mstab, -(bcum + mstab)), rep)
        ct_s[c] = cc.T
        wst_s[c] = jnp.exp(btot[c] + cc - mnew).T
        rows_s[0, c:c + 1, :] = mprev
        rows_s[1, c:c + 1, :] = jnp.exp(btot[c] + mprev - mnew)
    _, rep_c = gate_lanes(rows_s.shape[1])
    mpb = _dot(_pieces((rows_s[0],), rep_c), selm_ref[...])
    dec = _dot(_pieces((rows_s[1],), rep_c), seld_ref[...])
    for c in range(nc):
        mpb_s[c] = mpb[c:c + 1, :]
        dec_s[c] = dec[c:c + 1, :]

    t_i = lax.broadcasted_iota(jnp.int32, (L, L), 0)
    s_i = lax.broadcasted_iota(jnp.int32, (L, L), 1)
    low = lax.broadcasted_iota(jnp.int32, (L, LANES), 1) < ML_DV
    ones_e = jnp.where(low, 1.0, 0.0).astype(BF16)
    ones_o = jnp.where(low, 0.0, 1.0).astype(BF16)
    blk = 4 * LANES

    def finish(ci, p, hp):
        rows = slice(ci * L, (ci + 1) * L)
        sl = slice(p * LANES, (p + 1) * LANES)
        sq = hp * hp
        s_e = jnp.sum(jnp.where(low, sq, 0.0), axis=-1, keepdims=True)
        s_o = jnp.sum(jnp.where(low, 0.0, sq), axis=-1, keepdims=True)
        ms = jnp.where(low, s_e, s_o) * (1.0 / ML_DV)
        out = hp * lax.rsqrt(ms + EPS) * ghead_ref[:, sl] * so_ref[0, rows, sl].astype(F32)
        o_ref[0, rows, sl] = out.astype(BF16)

    def chunk_dir(ci, d, first):
        rows = slice(ci * L, (ci + 1) * L)
        allowed = t_i >= s_i if d == 0 else t_i <= s_i
        z = _dot(a_s[rows, :], sela_ref[:, d * ML_PAIRS * blk:(d + 1) * ML_PAIRS * blk])
        for p in range(ML_PAIRS):
            le = _gate_lane(d, 0, 2 * p)
            lo = le + 1
            dp = d * ML_PAIRS + p
            sl = slice(p * LANES, (p + 1) * LANES)
            zb = z[:, p * blk:(p + 1) * blk]
            qp = q_s[rows, sl]
            ktp = kt_s[ci, sl, :]
            vp = v_ref[0, rows, sl]
            zk = jnp.zeros((ML_DK, L), BF16)
            yt = jnp.concatenate([jnp.concatenate([ktp[:ML_DK], zk], axis=0),
                                  jnp.concatenate([zk, ktp[ML_DK:]], axis=0)], axis=1)
            qk = _dot(qp, yt)
            e_e = jnp.where(allowed, zb[:, 0:LANES] + ct_s[ci, le:le + 1, :], -jnp.inf)
            e_o = jnp.where(allowed, zb[:, LANES:2 * LANES] + ct_s[ci, lo:lo + 1, :], -jnp.inf)
            e_q = zb[:, 2 * LANES:3 * LANES] + mpb_s[ci, :, dp * LANES:(dp + 1) * LANES]
            pw = jnp.exp(jnp.concatenate([e_e, e_o, e_q], axis=1))
            x = (pw * jnp.concatenate([qk, qp.astype(F32)], axis=1)).astype(BF16)
            zero = jnp.zeros_like(vp)
            v2 = jnp.concatenate([jnp.concatenate([jnp.where(low, vp, zero), ones_e], axis=1),
                                  jnp.concatenate([jnp.where(low, zero, vp), ones_o], axis=1)], axis=0)
            cst = c_s[d, p]
            r = _dot(x, jnp.concatenate([v2, cst.astype(BF16)], axis=0))
            clamp = jnp.exp(zb[:, 3 * LANES:4 * LANES])
            hdir = r[:, 0:LANES] / jnp.maximum(jnp.abs(r[:, LANES:2 * LANES]), clamp)
            if first:
                h_s[rows, sl] = hdir
            else:
                finish(ci, p, h_s[rows, sl] + hdir)
            wsrow = jnp.concatenate([wst_s[ci, le:le + 1, :], wst_s[ci, lo:lo + 1, :]], axis=1)
            upd = _dot((yt.astype(F32) * wsrow).astype(BF16), v2)
            drow = dec_s[ci, :, dp * blk:(dp + 1) * blk]
            dmat = jnp.concatenate([jnp.broadcast_to(drow[:, 0:2 * LANES], (ML_DK, 2 * LANES)),
                                    jnp.broadcast_to(drow[:, 2 * LANES:4 * LANES], (ML_DK, 2 * LANES))], axis=0)
            c_s[d, p] = cst * dmat + upd

    conv_chunk(0)
    conv_chunk(nc - 1)
    for j in range(nc):
        first = j < nc - 1 - j
        chunk_dir(j, 0, first)
        chunk_dir(nc - 1 - j, 1, first)
        if j + 1 < nc - 2 - j:
            conv_chunk(j + 1)
            conv_chunk(nc - 2 - j)


def _mlstm(u, v, so, g, conv_w, wq_bd, wkt_bd, g_head):
    B, S, W = u.shape
    L = ML_CHUNK
    nc = S // L
    sels = _mlstm_selectors()
    seq = lambda w: pl.BlockSpec((1, S, w), lambda b: (b, 0, 0))
    consts = (conv_w, wq_bd, wkt_bd, g_head) + sels
    return pl.pallas_call(
        _mlstm_kernel, grid=(B,),
        in_specs=[seq(W), seq(W), seq(W), seq(2 * LANES)] + [_const_spec(a.shape) for a in consts],
        out_specs=seq(W),
        out_shape=jax.ShapeDtypeStruct((B, S, W), BF16),
        scratch_shapes=[
            pltpu.VMEM((S + 2 * CONV_PAD, W), F32),
            pltpu.VMEM((S, W), F32),
            pltpu.VMEM((S, W), BF16),
            pltpu.VMEM((nc, W, L), BF16),
            pltpu.VMEM((3, S, LANES), F32),
            pltpu.VMEM((S, LANES), BF16),
            pltpu.VMEM((nc, LANES, L), F32),
            pltpu.VMEM((nc, LANES, L), F32),
            pltpu.VMEM((2, -(-nc // 16) * 16, LANES), F32),
            pltpu.VMEM((nc, 1, 2 * ML_PAIRS * LANES), F32),
            pltpu.VMEM((nc, 1, 2 * ML_PAIRS * 4 * LANES), F32),
            pltpu.VMEM((2, ML_PAIRS, LANES, 2 * LANES), F32),
        ],
        compiler_params=pltpu.CompilerParams(dimension_semantics=("parallel",),
                                             vmem_limit_bytes=VMEM_LIMIT),
        name="mlstm",
    )(u, v, so, g, *consts)


def _memkv_kernel(mem_ref, g_ref, w_ref, k_ref, v_ref):
    mn = _rms(mem_ref[0], g_ref[...]).astype(BF16)
    kv = _dot(mn, w_ref[...])
    k_ref[0] = kv[:, :MEM_WIDTH].astype(BF16)
    v_ref[0] = kv[:, MEM_WIDTH:].astype(BF16)


def _memkv(mem, mem_g, w_kv):
    B, M, D = mem.shape
    blk = lambda w: pl.BlockSpec((1, M, w), lambda b: (b, 0, 0))
    return pl.pallas_call(
        _memkv_kernel, grid=(B,),
        in_specs=[blk(D), _const_spec(mem_g.shape), _const_spec(w_kv.shape)],
        out_specs=(blk(MEM_WIDTH), blk(MEM_WIDTH)),
        out_shape=(jax.ShapeDtypeStruct((B, M, MEM_WIDTH), BF16),) * 2,
        compiler_params=pltpu.CompilerParams(dimension_semantics=("parallel",)),
        name="memkv",
    )(mem, mem_g, w_kv)


def _memattn_kernel(q_ref, k_ref, v_ref, o_ref):
    for hh in range(MEM_HEADS):
        sl = slice(hh * MEM_HEAD_DIM, (hh + 1) * MEM_HEAD_DIM)
        s = _dot_nt(q_ref[0, :, sl], k_ref[0, :, sl])
        e = jnp.exp2(s - jnp.max(s, axis=-1, keepdims=True))
        den = jnp.sum(e, axis=-1, keepdims=True)
        o_ref[0, :, sl] = (_dot(e.astype(BF16), v_ref[0, :, sl]) / den).astype(BF16)


def _memattn(q, k, v, tq):
    B, S, W = q.shape
    M = k.shape[1]
    return pl.pallas_call(
        _memattn_kernel, grid=(B, S // tq),
        in_specs=[pl.BlockSpec((1, tq, W), lambda b, i: (b, i, 0)),
                  pl.BlockSpec((1, M, W), lambda b, i: (b, 0, 0)),
                  pl.BlockSpec((1, M, W), lambda b, i: (b, 0, 0))],
        out_specs=pl.BlockSpec((1, tq, W), lambda b, i: (b, i, 0)),
        out_shape=jax.ShapeDtypeStruct((B, S, W), BF16),
        compiler_params=pltpu.CompilerParams(dimension_semantics=("parallel", "parallel")),
        name="memattn",
    )(q, k, v)


def _out_kernel(x_ref, gates_ref, omla_ref, oml_ref, omem_ref, wb_ref, wout_ref, gffn_ref,
                wg_ref, wu_ref, wd_ref, gfin_ref, o_ref):
    merged = None
    for b, oref in enumerate((omla_ref, oml_ref, omem_ref)):
        y = _dot(oref[...], wb_ref[b]) * gates_ref[:, b * D_MODEL:(b + 1) * D_MODEL].astype(F32)
        merged = y if merged is None else merged + y
    x1 = x_ref[...] + _dot(merged.astype(BF16), wout_ref[...])
    h2 = _rms(x1, gffn_ref[...]).astype(BF16)
    hg = _dot(h2, wg_ref[...])
    a = (hg * _sigmoid(hg) * _dot(h2, wu_ref[...])).astype(BF16)
    x2 = x1 + _dot(a, wd_ref[...])
    o_ref[...] = _rms(x2, gfin_ref[...])


def _out_block(x, gates, o_mla, o_ml, o_mem, w_branch, w_out, g_ffn, w_g, w_u, w_d, g_final, tm):
    T, D = x.shape
    tok = lambda w: pl.BlockSpec((tm, w), lambda i: (i, 0))
    consts = (w_branch, w_out, g_ffn, w_g, w_u, w_d, g_final)
    return pl.pallas_call(
        _out_kernel, grid=(T // tm,),
        in_specs=[tok(D), tok(N_BRANCH * D), tok(o_mla.shape[1]), tok(o_ml.shape[1]), tok(o_mem.shape[1])]
        + [_const_spec(a.shape) for a in consts],
        out_specs=tok(D),
        out_shape=jax.ShapeDtypeStruct((T, D), x.dtype),
        compiler_params=pltpu.CompilerParams(dimension_semantics=("parallel",),
                                             vmem_limit_bytes=VMEM_LIMIT),
        name="out_block",
    )(x, gates, o_mla, o_ml, o_mem, *consts)


def _prep_in_weights(w_in, gate_bias):
    sizes = (N_BRANCH * D_MODEL, MLA_Q_RANK, MLA_KV_RANK, MLA_ROPE, ML_WIDTH, ML_WIDTH, ML_WIDTH,
             N_ML_GATES, MEM_WIDTH)
    offs = np.cumsum((0,) + sizes)
    w_gate, w_cq, w_ckv, w_kr, w_mu, w_mv, w_mo, w_mg, w_memq = [
        w_in[:, offs[i]:offs[i + 1]] for i in range(len(sizes))]
    half = MLA_ROPE // 2
    y1, y2 = w_kr[:, :half], w_kr[:, half:]
    z = lambda n: jnp.zeros((D_MODEL, n), w_in.dtype)
    w_kr = jnp.concatenate([z(MLA_NOPE), y1, y2, -y2, y1], axis=1)
    H = ML_HEADS
    pad = LANES - 2 * ML_DIR_LANES

    def gate_block(cols, fwd, bwd, zeros):
        return jnp.concatenate([cols(fwd)] * ML_REPL + [cols(bwd)] * ML_REPL + [zeros], axis=-1)

    wcols = lambda g: w_mg[:, g * H:(g + 1) * H]
    w_gi = gate_block(wcols, 0, 2, z(pad))
    w_gf = gate_block(wcols, 1, 3, z(pad))
    w_main = jnp.concatenate([w_gate, w_cq, w_ckv, w_kr, w_mu, w_mv, w_mo, w_memq, w_gi, w_gf],
                             axis=1).astype(BF16)
    bcols = lambda g: gate_bias[g]
    zb = jnp.zeros((pad,), gate_bias.dtype)
    gbias = jnp.concatenate([gate_block(bcols, 0, 2, zb), gate_block(bcols, 1, 3, zb)])[None, :]
    return w_main, gbias.astype(F32)


def _prep_mla_weights(w_uq, w_uk, w_uv):
    half = MLA_ROPE // 2
    qn = w_uq[:, :, :MLA_NOPE]
    x1 = w_uq[:, :, MLA_NOPE:MLA_NOPE + half]
    x2 = w_uq[:, :, MLA_NOPE + half:]
    wuq = jnp.concatenate([qn, x1, x2, -x2, x1], axis=-1).reshape(MLA_Q_RANK, MLA_HEADS * LANES)
    wuk = jnp.concatenate([w_uk, jnp.zeros_like(w_uk)], axis=-1).reshape(MLA_KV_RANK, MLA_HEADS * LANES).T
    wv = w_uv.reshape(MLA_KV_RANK, MLA_HEADS // 2, 2, MLA_V)
    zv = jnp.zeros_like(wv[:, :, 0])
    wuv = jnp.concatenate([wv[:, :, 0], zv, zv, wv[:, :, 1]], axis=-1).reshape(MLA_KV_RANK, MLA_HEADS * LANES)
    return wuq.astype(BF16), wuk.astype(BF16), wuv.astype(BF16)


def _block_diag(w):
    H, d, _ = w.shape
    eye = jnp.eye(H, dtype=w.dtype)
    return (eye[:, None, :, None] * w[:, :, None, :]).reshape(H * d, H * d)


def _layer(x, mem, positions, g_mix, w_in, mla_g_q, mla_g_kv, mla_w_uq, mla_w_uk, mla_w_uv,
           ml_conv_w, ml_w_q, ml_w_k, ml_gate_bias, ml_g_head, mem_g, mem_w_kv,
           w_branch, w_out, g_ffn, w_ffn_gate, w_ffn_up, w_ffn_down, g_final):
    B, S, D = x.shape
    row = lambda g: g.reshape(1, -1).astype(F32)
    w_main, gbias = _prep_in_weights(w_in, ml_gate_bias)
    wuq, wuk, wuv = _prep_mla_weights(mla_w_uq, mla_w_uk, mla_w_uv)
    half = MLA_ROPE // 2
    inv = ROPE_THETA ** (-jnp.arange(half, dtype=F32) / half)
    phase = np.where(np.arange(LANES) % MLA_NOPE < MLA_ROPE, 0.0, 0.5 * np.pi).astype(np.float32)
    invf = jnp.stack([jnp.tile(inv, LANES // half), jnp.asarray(phase)])

    gates, q, k, v, mu, mv, mo, mg, memq = _inproj(
        x, positions.reshape(B, S, 1), row(g_mix), w_main, gbias, row(mla_g_q), row(mla_g_kv),
        wuq, wuk, wuv, invf, tm=min(512, S))
    o_mla = _mla_attention(q, k, v)
    o_ml = _mlstm(mu, mv, mo, mg, ml_conv_w.astype(F32), _block_diag(ml_w_q).astype(BF16),
                  _block_diag(ml_w_k * (ML_DK ** -0.5)).T.astype(BF16), row(ml_g_head))
    mk, mvv = _memkv(mem, row(mem_g), mem_w_kv.astype(BF16))
    o_mem = _memattn(memq, mk, mvv, tq=min(512, S))

    T = B * S
    flat = lambda a: a.reshape(T, a.shape[-1])
    out = _out_block(flat(x), flat(gates), flat(o_mla), flat(o_ml), flat(o_mem),
                     w_branch.astype(BF16), w_out.astype(BF16), row(g_ffn), w_ffn_gate.astype(BF16),
                     w_ffn_up.astype(BF16), w_ffn_down.astype(BF16), row(g_final), tm=min(512, T))
    return out.reshape(B, S, D)


def kernel(x, mem, positions, g_mix, w_in, mla_g_q, mla_g_kv, mla_w_uq, mla_w_uk, mla_w_uv, ml_conv_w,
           ml_w_q, ml_w_k, ml_gate_bias, ml_g_head, mem_g, mem_w_kv, w_branch, w_out, g_ffn,
           w_ffn_gate, w_ffn_up, w_ffn_down, g_final):
    depth = g_mix.shape[0]
    assert depth == 1, "the final norm is fused into the single layer's output kernel"
    return _layer(x, mem, positions, g_mix[0], w_in[0], mla_g_q[0], mla_g_kv[0], mla_w_uq[0],
                  mla_w_uk[0], mla_w_uv[0], ml_conv_w[0], ml_w_q[0], ml_w_k[0], ml_gate_bias[0],
                  ml_g_head[0], mem_g[0], mem_w_kv[0], w_branch[0], w_out[0], g_ffn[0],
                  w_ffn_gate[0], w_ffn_up[0], w_ffn_down[0], g_final)
```

```python
import math

import numpy as np
import jax
import jax.numpy as jnp
from jax import lax
from jax.experimental import pallas as pl
from jax.experimental.pallas import tpu as pltpu

D_MODEL = 1024
MEM_LEN = 256
EPS = 1e-6
MLA_HEADS = 8
MLA_NOPE = 64
MLA_ROPE = 32
MLA_V = 64
MLA_Q_RANK = 384
MLA_KV_RANK = 256
ROPE_THETA = 10000.0
ML_HEADS = 8
ML_DK = 64
ML_DV = 64
ML_WIDTH = ML_HEADS * ML_DV
CONV_WIDTH = 5
MEM_HEADS = 4
MEM_HEAD_DIM = 128
MEM_WIDTH = MEM_HEADS * MEM_HEAD_DIM
N_BRANCH = 3
N_ML_GATES = 4 * ML_HEADS
D_FF = 2816

LANES = 128
MLA_UNIT_ROWS = 512
MLA_STEP_HEADS = 4
ML_CHUNK = LANES
ML_PAIRS = ML_HEADS // 2
ML_REPL = 6
ML_DIR_LANES = ML_REPL * ML_HEADS
NEG_INIT = -1e30
CONV_PAD = 8
VMEM_LIMIT = 56 * 1024 * 1024
INPROJ_ROWS = 512
OUT_ROWS = 512
MEMATTN_ROWS = 1024

BF16 = jnp.bfloat16
F32 = jnp.float32

_C_GATE = 0
_C_CQ = _C_GATE + N_BRANCH * D_MODEL
_C_CKV = _C_CQ + MLA_Q_RANK
_C_KR = _C_CKV + MLA_KV_RANK
_C_MU = _C_KR + LANES
_C_MV = _C_MU + ML_WIDTH
_C_MO = _C_MV + ML_WIDTH
_C_MEMQ = _C_MO + ML_WIDTH
_C_MGI = _C_MEMQ + MEM_WIDTH
_C_MGF = _C_MGI + LANES
_C_END = _C_MGF + LANES


def _const_spec(shape):
    nd = len(shape)
    return pl.BlockSpec(shape, lambda *_: (0,) * nd, pipeline_mode=pl.Buffered(1))


def _sigmoid(x):
    return 1.0 / (1.0 + jnp.exp(-x))


def _rms(x, g):
    return x * lax.rsqrt(jnp.mean(x * x, axis=-1, keepdims=True) + EPS) * g


def _dot(a, b):
    return jnp.dot(a, b, preferred_element_type=F32)


def _dot_nt(a, b):
    return lax.dot_general(a, b, (((1,), (1,)), ((), ())), preferred_element_type=F32)


def _inproj_kernel(x_ref, pos_ref, gmix_ref, w_ref, gbias_ref, gq_ref, gkv_ref, wuq_ref, wuk_ref,
                   wuv_ref, rope_ref,
                   gates_ref, q_ref, k_ref, v_ref, mu_ref, mv_ref, mo_ref, mg_ref, memq_ref):
    x = x_ref[0]
    tm = x.shape[0]
    h = _rms(x, gmix_ref[...]).astype(BF16)

    def proj(a, b):
        return _dot(h, w_ref[:, a:b])

    cqn = _rms(proj(_C_CQ, _C_CKV), gq_ref[...]).astype(BF16)
    ckvn = _rms(proj(_C_CKV, _C_KR), gkv_ref[...]).astype(BF16)
    qe = _dot(cqn, wuq_ref[...])
    ket = _dot_nt(wuk_ref[...], ckvn)
    ve = _dot(ckvn, wuv_ref[...])

    mu_ref[0] = proj(_C_MU, _C_MV)
    mv_ref[0] = proj(_C_MV, _C_MO).astype(BF16)
    mo_ref[0] = _sigmoid(proj(_C_MO, _C_MEMQ)).astype(BF16)
    memq_ref[0] = (proj(_C_MEMQ, _C_MGI) * (MEM_HEAD_DIM ** -0.5 * math.log2(math.e))).astype(BF16)
    mg_ref[0] = proj(_C_MGI, _C_END) + gbias_ref[...]

    lane = lax.broadcasted_iota(jnp.int32, (tm, LANES), 1)
    hm = tm // 2
    pos = pos_ref[0].astype(F32)
    pos2 = jnp.where(lax.broadcasted_iota(jnp.int32, (hm, LANES), 1) < MLA_NOPE, pos[:hm], pos[hm:])
    rope2 = jnp.cos(pos2 * rope_ref[0:1, :] - rope_ref[1:2, :])
    rope = jnp.concatenate([pltpu.roll(rope2, MLA_NOPE, 1), rope2], axis=0)
    kra = proj(_C_KR, _C_MU) * rope
    kr = jnp.where(lane < MLA_NOPE, 0.0,
                   kra + pltpu.roll(kra, MLA_ROPE, 1) + pltpu.roll(kra, LANES - MLA_ROPE, 1))
    q_scale = (MLA_NOPE + MLA_ROPE) ** -0.5 * math.log2(math.e)
    qmul = jnp.where(lane < MLA_NOPE, 1.0, rope) * q_scale
    krt = kr.T
    for hh in range(MLA_HEADS):
        sl = slice(hh * LANES, (hh + 1) * LANES)
        q_ref[0, hh] = (qe[:, sl] * qmul).astype(BF16)
        k_ref[0, hh] = (ket[sl, :] + krt).astype(BF16)
        v_ref[0, hh] = jnp.where(lane == (MLA_V if hh % 2 == 0 else 0), 1.0, ve[:, sl]).astype(BF16)

    gates_ref[0] = _sigmoid(proj(_C_GATE, _C_CQ)).astype(BF16)


def _inproj(x, pos, g_mix, w_main, gbias, g_q, g_kv, wuq, wuk, wuv, invf, tm):
    B, S, D = x.shape
    grid = (B, S // tm)
    tok = lambda w: pl.BlockSpec((1, tm, w), lambda b, i: (b, i, 0))
    head = pl.BlockSpec((1, MLA_HEADS, tm, LANES), lambda b, i: (b, 0, i, 0))
    head_t = pl.BlockSpec((1, MLA_HEADS, LANES, tm), lambda b, i: (b, 0, 0, i))
    out_shape = (
        jax.ShapeDtypeStruct((B, S, N_BRANCH * D_MODEL), BF16),
        jax.ShapeDtypeStruct((B, MLA_HEADS, S, LANES), BF16),
        jax.ShapeDtypeStruct((B, MLA_HEADS, LANES, S), BF16),
        jax.ShapeDtypeStruct((B, MLA_HEADS, S, LANES), BF16),
        jax.ShapeDtypeStruct((B, S, ML_WIDTH), F32),
        jax.ShapeDtypeStruct((B, S, ML_WIDTH), BF16),
        jax.ShapeDtypeStruct((B, S, ML_WIDTH), BF16),
        jax.ShapeDtypeStruct((B, S, 2 * LANES), F32),
        jax.ShapeDtypeStruct((B, S, MEM_WIDTH), BF16),
    )
    out_specs = (tok(N_BRANCH * D_MODEL), head, head_t, head, tok(ML_WIDTH), tok(ML_WIDTH),
                 tok(ML_WIDTH), tok(2 * LANES), tok(MEM_WIDTH))
    consts = (g_mix, w_main, gbias, g_q, g_kv, wuq, wuk, wuv, invf)
    in_specs = [tok(D), pl.BlockSpec((1, tm, 1), lambda b, i: (b, i, 0))] + [_const_spec(a.shape) for a in consts]
    return pl.pallas_call(
        _inproj_kernel, grid=grid, in_specs=in_specs, out_specs=out_specs, out_shape=out_shape,
        compiler_params=pltpu.CompilerParams(dimension_semantics=("parallel", "parallel"),
                                             vmem_limit_bytes=VMEM_LIMIT),
        name="inproj",
    )(x, pos, *consts)


def _mla_kernel(q_ref, kt_ref, v_ref, o_ref, s_buf, p_buf):
    S = q_ref.shape[2]
    tu = s_buf.shape[1]
    low = lax.broadcasted_iota(jnp.int32, (tu, LANES), 1) < MLA_V
    unit = 0
    for pair in range(MLA_STEP_HEADS // 2):
        for t in range(S // tu):
            rows = slice(t * tu, (t + 1) * tu)
            outs = []
            for j in range(2):
                hh = 2 * pair + j
                slot = unit % 2
                unit += 1
                s_buf[slot] = _dot(q_ref[0, hh, rows, :], kt_ref[0, hh])
                s = s_buf[slot]
                p_buf[slot] = jnp.exp2(s - jnp.max(s, axis=-1, keepdims=True)).astype(BF16)
                r = _dot(p_buf[slot], v_ref[0, hh])
                den = r[:, MLA_V:MLA_V + 1] if j == 0 else r[:, 0:1]
                outs.append(r / den)
            o_ref[0, rows, pair * LANES:(pair + 1) * LANES] = jnp.where(low, outs[0], outs[1]).astype(BF16)


def _mla_attention(q, kt, v):
    B, H, S, _ = q.shape
    hs = MLA_STEP_HEADS
    tu = min(MLA_UNIT_ROWS, S)
    heads = lambda shape: pl.BlockSpec((1, hs) + shape, lambda b, p: (b, p, 0, 0))
    return pl.pallas_call(
        _mla_kernel, grid=(B, H // hs),
        in_specs=[heads((S, LANES)), heads((LANES, S)), heads((S, LANES))],
        out_specs=pl.BlockSpec((1, S, hs * MLA_V), lambda b, p: (b, 0, p)),
        out_shape=jax.ShapeDtypeStruct((B, S, H * MLA_V), BF16),
        scratch_shapes=[pltpu.VMEM((2, tu, S), F32),
                        pltpu.VMEM((2, tu, S), BF16)],
        compiler_params=pltpu.CompilerParams(dimension_semantics=("parallel", "parallel"),
                                             vmem_limit_bytes=VMEM_LIMIT),
        name="mla_attn",
    )(q, kt, v)


def _gate_lane(d, r, h):
    return ML_DIR_LANES * d + ML_HEADS * r + h


def _split3(x):
    hi = x.astype(BF16).astype(F32)
    r = x - hi
    mid = r.astype(BF16).astype(F32)
    lo = (r - mid).astype(BF16).astype(F32)
    return hi, mid, lo


def _pieces(srcs, rep):
    out = None
    for i, src in enumerate(srcs):
        for j, piece in enumerate(_split3(src)):
            out = piece if out is None else jnp.where(rep == 3 * i + j, piece, out)
    return out.astype(BF16)


def _cummax_dirs(x, fwd_lane):
    n = x.shape[0]
    row = lax.broadcasted_iota(jnp.int32, x.shape, 0)
    sh = 1
    while sh < n:
        if sh % 8 == 0:
            pad = jnp.full((sh, x.shape[1]), -jnp.inf, x.dtype)
            up = jnp.concatenate([pad, x[:n - sh]], axis=0)
            dn = jnp.concatenate([x[sh:], pad], axis=0)
        else:
            up = jnp.where(row >= sh, pltpu.roll(x, sh, 0), -jnp.inf)
            dn = jnp.where(row < n - sh, pltpu.roll(x, n - sh, 0), -jnp.inf)
        x = jnp.maximum(x, jnp.where(fwd_lane, up, dn))
        sh *= 2
    return x


def _mlstm_selectors():
    sel_a = np.zeros((LANES, 2, ML_PAIRS, 4 * LANES), np.float32)
    sel_m = np.zeros((LANES, 2, ML_PAIRS, LANES), np.float32)
    sel_d = np.zeros((LANES, 2, ML_PAIRS, 4 * LANES), np.float32)
    for d in range(2):
        for p in range(ML_PAIRS):
            for j in range(2):
                h = 2 * p + j
                for r in range(3):
                    a = _gate_lane(d, r, h)
                    g = _gate_lane(d, r + 3, h)
                    sel_a[a, d, p, j * LANES:(j + 1) * LANES] = 1
                    sel_a[a, d, p, 2 * LANES + j * ML_DK:2 * LANES + (j + 1) * ML_DK] = 1
                    sel_a[g, d, p, 3 * LANES + j * ML_DV:3 * LANES + (j + 1) * ML_DV] = 1
                    sel_m[a, d, p, j * ML_DK:(j + 1) * ML_DK] = 1
                    sel_d[a, d, p, j * 2 * LANES:(j + 1) * 2 * LANES] = 1
    as_bf = lambda m: jnp.asarray(m.reshape(LANES, -1), BF16)
    return as_bf(sel_a), as_bf(sel_m), as_bf(sel_d)


def _mlstm_kernel(u_ref, v_ref, so_ref, g_ref, cw_ref, wq_ref, wkt_ref, ghead_ref, sela_ref, selm_ref,
                  seld_ref, o_ref,
                  upad, h_s, q_s, kt_s, gate_s, a_s, ct_s, wst_s, rows_s, mpb_s, dec_s, c_s):
    S = u_ref.shape[1]
    L = ML_CHUNK
    nc = S // L
    half = CONV_WIDTH // 2

    zpad = jnp.zeros((CONV_PAD, ML_WIDTH), F32)
    upad[0:CONV_PAD, :] = zpad
    upad[CONV_PAD + S:CONV_PAD + S + CONV_PAD, :] = zpad
    upad[CONV_PAD:CONV_PAD + S, :] = u_ref[0]
    c_s[...] = jnp.zeros(c_s.shape, F32)
    rows_s[...] = jnp.zeros(rows_s.shape, F32)

    def conv_chunk(r):
        acc = jnp.zeros((L, ML_WIDTH), F32)
        for w in range(CONV_WIDTH):
            start = CONV_PAD - half + w + r * L
            acc = acc + upad[start:start + L, :] * cw_ref[w:w + 1, :]
        c = (acc * _sigmoid(acc)).astype(BF16)
        q_s[r * L:(r + 1) * L, :] = _dot(c, wq_ref[...]).astype(BF16)
        kt_s[r] = _dot_nt(wkt_ref[...], c).astype(BF16)

    def gate_lanes(nrows):
        lane = lax.broadcasted_iota(jnp.int32, (nrows, LANES), 1)
        fwd = lane < ML_DIR_LANES
        return fwd, jnp.right_shift(jnp.where(fwd, lane, lane - ML_DIR_LANES), 3)

    fwd_lane, rep = gate_lanes(L)
    fwd_row, _ = gate_lanes(1)
    tri_l = jnp.where(lax.broadcasted_iota(jnp.int32, (L, L), 0) >= lax.broadcasted_iota(jnp.int32, (L, L), 1),
                      1.0, 0.0).astype(BF16)
    btot, gmax = [], []
    for c in range(nc):
        rs = slice(c * L, (c + 1) * L)
        gf = g_ref[0, rs, LANES:2 * LANES]
        lf = jnp.minimum(gf, 0.0) - jnp.log1p(jnp.exp(-jnp.abs(gf)))
        bf = sum(_dot(tri_l, piece.astype(BF16)) for piece in _split3(lf))
        bt = bf[L - 1:L, :]
        bcum = jnp.where(fwd_lane, bf, bt - bf + lf)
        cc = g_ref[0, rs, 0:LANES] - bcum
        cmx = _cummax_dirs(cc, fwd_lane)
        gate_s[0, rs, :] = bcum
        gate_s[1, rs, :] = cc
        gate_s[2, rs, :] = cmx
        btot.append(bt)
        gmax.append(jnp.where(fwd_row, cmx[L - 1:L, :], cmx[0:1, :]))

    def scan_m(order):
        prev, new = [None] * nc, [None] * nc
        m = jnp.full((1, LANES), NEG_INIT, F32)
        for c in order:
            prev[c] = m
            m = jnp.maximum(btot[c] + m, btot[c] + gmax[c])
            new[c] = m
        return prev, new

    prev_f, new_f = scan_m(range(nc))
    prev_b, new_b = scan_m(reversed(range(nc)))

    for c in range(nc):
        rs = slice(c * L, (c + 1) * L)
        mprev = jnp.where(fwd_row, prev_f[c], prev_b[c])
        mnew = jnp.where(fwd_row, new_f[c], new_b[c])
        bcum, cc, cmx = gate_s[0, rs, :], gate_s[1, rs, :], gate_s[2, rs, :]
        mstab = jnp.maximum(mprev, cmx)
        a_s[rs, :] = _pieces((-mstab, -(bcum + mstab)), rep)
        ct_s[c] = cc.T
        wst_s[c] = jnp.exp(btot[c] + cc - mnew).T
        rows_s[0, c:c + 1, :] = mprev
        rows_s[1, c:c + 1, :] = jnp.exp(btot[c] + mprev - mnew)
    _, rep_c = gate_lanes(rows_s.shape[1])
    mpb = _dot(_pieces((rows_s[0],), rep_c), selm_ref[...])
    dec = _dot(_pieces((rows_s[1],), rep_c), seld_ref[...])
    for c in range(nc):
        mpb_s[c] = mpb[c:c + 1, :]
        dec_s[c] = dec[c:c + 1, :]

    t_i = lax.broadcasted_iota(jnp.int32, (L, L), 0)
    s_i = lax.broadcasted_iota(jnp.int32, (L, L), 1)
    low = lax.broadcasted_iota(jnp.int32, (L, LANES), 1) < ML_DV
    ones_e = jnp.where(low, 1.0, 0.0).astype(BF16)
    ones_o = jnp.where(low, 0.0, 1.0).astype(BF16)
    blk = 4 * LANES

    def finish(ci, p, hp):
        rows = slice(ci * L, (ci + 1) * L)
        sl = slice(p * LANES, (p + 1) * LANES)
        sq = hp * hp
        s_e = jnp.sum(jnp.where(low, sq, 0.0), axis=-1, keepdims=True)
        s_o = jnp.sum(jnp.where(low, 0.0, sq), axis=-1, keepdims=True)
        ms = jnp.where(low, s_e, s_o) * (1.0 / ML_DV)
        out = hp * lax.rsqrt(ms + EPS) * ghead_ref[:, sl] * so_ref[0, rows, sl].astype(F32)
        o_ref[0, rows, sl] = out.astype(BF16)

    def chunk_dir(ci, d, first):
        rows = slice(ci * L, (ci + 1) * L)
        allowed = t_i >= s_i if d == 0 else t_i <= s_i
        z = _dot(a_s[rows, :], sela_ref[:, d * ML_PAIRS * blk:(d + 1) * ML_PAIRS * blk])
        for p in range(ML_PAIRS):
            le = _gate_lane(d, 0, 2 * p)
            lo = le + 1
            dp = d * ML_PAIRS + p
            sl = slice(p * LANES, (p + 1) * LANES)
            zb = z[:, p * blk:(p + 1) * blk]
            qp = q_s[rows, sl]
            ktp = kt_s[ci, sl, :]
            vp = v_ref[0, rows, sl]
            zk = jnp.zeros((ML_DK, L), BF16)
            yt = jnp.concatenate([jnp.concatenate([ktp[:ML_DK], zk], axis=0),
                                  jnp.concatenate([zk, ktp[ML_DK:]], axis=0)], axis=1)
            qk = _dot(qp, yt)
            e_e = jnp.where(allowed, zb[:, 0:LANES] + ct_s[ci, le:le + 1, :], -jnp.inf)
            e_o = jnp.where(allowed, zb[:, LANES:2 * LANES] + ct_s[ci, lo:lo + 1, :], -jnp.inf)
            e_q = zb[:, 2 * LANES:3 * LANES] + mpb_s[ci, :, dp * LANES:(dp + 1) * LANES]
            pw = jnp.exp(jnp.concatenate([e_e, e_o, e_q], axis=1))
            x = (pw * jnp.concatenate([qk, qp.astype(F32)], axis=1)).astype(BF16)
            zero = jnp.zeros_like(vp)
            v2 = jnp.concatenate([jnp.concatenate([jnp.where(low, vp, zero), ones_e], axis=1),
                                  jnp.concatenate([jnp.where(low, zero, vp), ones_o], axis=1)], axis=0)
            cst = c_s[d, p]
            r = _dot(x, jnp.concatenate([v2, cst.astype(BF16)], axis=0))
            clamp = jnp.exp(zb[:, 3 * LANES:4 * LANES])
            hdir = r[:, 0:LANES] / jnp.maximum(jnp.abs(r[:, LANES:2 * LANES]), clamp)
            if first:
                h_s[rows, sl] = hdir
            else:
                finish(ci, p, h_s[rows, sl] + hdir)
            wsrow = jnp.concatenate([wst_s[ci, le:le + 1, :], wst_s[ci, lo:lo + 1, :]], axis=1)
            upd = _dot((yt.astype(F32) * wsrow).astype(BF16), v2)
            drow = dec_s[ci, :, dp * blk:(dp + 1) * blk]
            dmat = jnp.concatenate([jnp.broadcast_to(drow[:, 0:2 * LANES], (ML_DK, 2 * LANES)),
                                    jnp.broadcast_to(drow[:, 2 * LANES:4 * LANES], (ML_DK, 2 * LANES))], axis=0)
            c_s[d, p] = cst * dmat + upd

    conv_chunk(0)
    conv_chunk(nc - 1)
    for j in range(nc):
        first = j < nc - 1 - j
        chunk_dir(j, 0, first)
        chunk_dir(nc - 1 - j, 1, first)
        if j + 1 < nc - 2 - j:
            conv_chunk(j + 1)
            conv_chunk(nc - 2 - j)


def _mlstm(u, v, so, g, conv_w, wq_bd, wkt_bd, g_head):
    B, S, W = u.shape
    L = ML_CHUNK
    nc = S // L
    sels = _mlstm_selectors()
    seq = lambda w: pl.BlockSpec((1, S, w), lambda b: (b, 0, 0))
    consts = (conv_w, wq_bd, wkt_bd, g_head) + sels
    return pl.pallas_call(
        _mlstm_kernel, grid=(B,),
        in_specs=[seq(W), seq(W), seq(W), seq(2 * LANES)] + [_const_spec(a.shape) for a in consts],
        out_specs=seq(W),
        out_shape=jax.ShapeDtypeStruct((B, S, W), BF16),
        scratch_shapes=[
            pltpu.VMEM((S + 2 * CONV_PAD, W), F32),
            pltpu.VMEM((S, W), F32),
            pltpu.VMEM((S, W), BF16),
            pltpu.VMEM((nc, W, L), BF16),
            pltpu.VMEM((3, S, LANES), F32),
            pltpu.VMEM((S, LANES), BF16),
            pltpu.VMEM((nc, LANES, L), F32),
            pltpu.VMEM((nc, LANES, L), F32),
            pltpu.VMEM((2, -(-nc // 16) * 16, LANES), F32),
            pltpu.VMEM((nc, 1, 2 * ML_PAIRS * LANES), F32),
            pltpu.VMEM((nc, 1, 2 * ML_PAIRS * 4 * LANES), F32),
            pltpu.VMEM((2, ML_PAIRS, LANES, 2 * LANES), F32),
        ],
        compiler_params=pltpu.CompilerParams(dimension_semantics=("parallel",),
                                             vmem_limit_bytes=VMEM_LIMIT),
        name="mlstm",
    )(u, v, so, g, *consts)


def _memkv_kernel(mem_ref, g_ref, w_ref, k_ref, v_ref):
    mn = _rms(mem_ref[0], g_ref[...]).astype(BF16)
    kv = _dot(mn, w_ref[...])
    k_ref[0] = kv[:, :MEM_WIDTH].astype(BF16)
    v_ref[0] = kv[:, MEM_WIDTH:].astype(BF16)


def _memkv(mem, mem_g, w_kv):
    B, M, D = mem.shape
    blk = lambda w: pl.BlockSpec((1, M, w), lambda b: (b, 0, 0))
    return pl.pallas_call(
        _memkv_kernel, grid=(B,),
        in_specs=[blk(D), _const_spec(mem_g.shape), _const_spec(w_kv.shape)],
        out_specs=(blk(MEM_WIDTH), blk(MEM_WIDTH)),
        out_shape=(jax.ShapeDtypeStruct((B, M, MEM_WIDTH), BF16),) * 2,
        compiler_params=pltpu.CompilerParams(dimension_semantics=("parallel",)),
        name="memkv",
    )(mem, mem_g, w_kv)


def _memattn_kernel(q_ref, k_ref, v_ref, o_ref):
    for hh in range(MEM_HEADS):
        sl = slice(hh * MEM_HEAD_DIM, (hh + 1) * MEM_HEAD_DIM)
        s = _dot_nt(q_ref[0, :, sl], k_ref[0, :, sl])
        e = jnp.exp2(s - jnp.max(s, axis=-1, keepdims=True))
        den = jnp.sum(e, axis=-1, keepdims=True)
        o_ref[0, :, sl] = (_dot(e.astype(BF16), v_ref[0, :, sl]) / den).astype(BF16)


def _memattn(q, k, v, tq):
    B, S, W = q.shape
    M = k.shape[1]
    return pl.pallas_call(
        _memattn_kernel, grid=(B, S // tq),
        in_specs=[pl.BlockSpec((1, tq, W), lambda b, i: (b, i, 0)),
                  pl.BlockSpec((1, M, W), lambda b, i: (b, 0, 0)),
                  pl.BlockSpec((1, M, W), lambda b, i: (b, 0, 0))],
        out_specs=pl.BlockSpec((1, tq, W), lambda b, i: (b, i, 0)),
        out_shape=jax.ShapeDtypeStruct((B, S, W), BF16),
        compiler_params=pltpu.CompilerParams(dimension_semantics=("parallel", "parallel")),
        name="memattn",
    )(q, k, v)


def _out_kernel(x_ref, gates_ref, omla_ref, oml_ref, omem_ref, wb_ref, wout_ref, gffn_ref,
                wg_ref, wu_ref, wd_ref, gfin_ref, o_ref):
    merged = None
    for b, oref in enumerate((omla_ref, oml_ref, omem_ref)):
        y = _dot(oref[...], wb_ref[b]) * gates_ref[:, b * D_MODEL:(b + 1) * D_MODEL].astype(F32)
        merged = y if merged is None else merged + y
    x1 = x_ref[...] + _dot(merged.astype(BF16), wout_ref[...])
    h2 = _rms(x1, gffn_ref[...]).astype(BF16)
    hg = _dot(h2, wg_ref[...])
    a = (hg * _sigmoid(hg) * _dot(h2, wu_ref[...])).astype(BF16)
    x2 = x1 + _dot(a, wd_ref[...])
    o_ref[...] = _rms(x2, gfin_ref[...])


def _out_block(x, gates, o_mla, o_ml, o_mem, w_branch, w_out, g_ffn, w_g, w_u, w_d, g_final, tm):
    T, D = x.shape
    tok = lambda w: pl.BlockSpec((tm, w), lambda i: (i, 0))
    consts = (w_branch, w_out, g_ffn, w_g, w_u, w_d, g_final)
    return pl.pallas_call(
        _out_kernel, grid=(T // tm,),
        in_specs=[tok(D), tok(N_BRANCH * D), tok(o_mla.shape[1]), tok(o_ml.shape[1]), tok(o_mem.shape[1])]
        + [_const_spec(a.shape) for a in consts],
        out_specs=tok(D),
        out_shape=jax.ShapeDtypeStruct((T, D), x.dtype),
        compiler_params=pltpu.CompilerParams(dimension_semantics=("parallel",),
                                             vmem_limit_bytes=VMEM_LIMIT),
        name="out_block",
    )(x, gates, o_mla, o_ml, o_mem, *consts)


def _prep_in_weights(w_in, gate_bias):
    sizes = (N_BRANCH * D_MODEL, MLA_Q_RANK, MLA_KV_RANK, MLA_ROPE, ML_WIDTH, ML_WIDTH, ML_WIDTH,
             N_ML_GATES, MEM_WIDTH)
    offs = np.cumsum((0,) + sizes)
    w_gate, w_cq, w_ckv, w_kr, w_mu, w_mv, w_mo, w_mg, w_memq = [
        w_in[:, offs[i]:offs[i + 1]] for i in range(len(sizes))]
    half = MLA_ROPE // 2
    y1, y2 = w_kr[:, :half], w_kr[:, half:]
    z = lambda n: jnp.zeros((D_MODEL, n), w_in.dtype)
    w_kr = jnp.concatenate([z(MLA_NOPE), y1, y2, -y2, y1], axis=1)
    H = ML_HEADS
    pad = LANES - 2 * ML_DIR_LANES

    def gate_block(cols, fwd, bwd, zeros):
        return jnp.concatenate([cols(fwd)] * ML_REPL + [cols(bwd)] * ML_REPL + [zeros], axis=-1)

    wcols = lambda g: w_mg[:, g * H:(g + 1) * H]
    w_gi = gate_block(wcols, 0, 2, z(pad))
    w_gf = gate_block(wcols, 1, 3, z(pad))
    w_main = jnp.concatenate([w_gate, w_cq, w_ckv, w_kr, w_mu, w_mv, w_mo, w_memq, w_gi, w_gf],
                             axis=1).astype(BF16)
    bcols = lambda g: gate_bias[g]
    zb = jnp.zeros((pad,), gate_bias.dtype)
    gbias = jnp.concatenate([gate_block(bcols, 0, 2, zb), gate_block(bcols, 1, 3, zb)])[None, :]
    return w_main, gbias.astype(F32)


def _prep_mla_weights(w_uq, w_uk, w_uv):
    half = MLA_ROPE // 2
    qn = w_uq[:, :, :MLA_NOPE]
    x1 = w_uq[:, :, MLA_NOPE:MLA_NOPE + half]
    x2 = w_uq[:, :, MLA_NOPE + half:]
    wuq = jnp.concatenate([qn, x1, x2, -x2, x1], axis=-1).reshape(MLA_Q_RANK, MLA_HEADS * LANES)
    wuk = jnp.concatenate([w_uk, jnp.zeros_like(w_uk)], axis=-1).reshape(MLA_KV_RANK, MLA_HEADS * LANES).T
    wv = w_uv.reshape(MLA_KV_RANK, MLA_HEADS // 2, 2, MLA_V)
    zv = jnp.zeros_like(wv[:, :, 0])
    wuv = jnp.concatenate([wv[:, :, 0], zv, zv, wv[:, :, 1]], axis=-1).reshape(MLA_KV_RANK, MLA_HEADS * LANES)
    return wuq.astype(BF16), wuk.astype(BF16), wuv.astype(BF16)


def _block_diag(w):
    H, d, _ = w.shape
    eye = jnp.eye(H, dtype=w.dtype)
    return (eye[:, None, :, None] * w[:, :, None, :]).reshape(H * d, H * d)


def _layer(x, mem, positions, g_mix, w_in, mla_g_q, mla_g_kv, mla_w_uq, mla_w_uk, mla_w_uv,
           ml_conv_w, ml_w_q, ml_w_k, ml_gate_bias, ml_g_head, mem_g, mem_w_kv,
           w_branch, w_out, g_ffn, w_ffn_gate, w_ffn_up, w_ffn_down, g_final):
    B, S, D = x.shape
    row = lambda g: g.reshape(1, -1).astype(F32)
    w_main, gbias = _prep_in_weights(w_in, ml_gate_bias)
    wuq, wuk, wuv = _prep_mla_weights(mla_w_uq, mla_w_uk, mla_w_uv)
    half = MLA_ROPE // 2
    inv = ROPE_THETA ** (-jnp.arange(half, dtype=F32) / half)
    phase = np.where(np.arange(LANES) % MLA_NOPE < MLA_ROPE, 0.0, 0.5 * np.pi).astype(np.float32)
    invf = jnp.stack([jnp.tile(inv, LANES // half), jnp.asarray(phase)])

    gates, q, k, v, mu, mv, mo, mg, memq = _inproj(
        x, positions.reshape(B, S, 1), row(g_mix), w_main, gbias, row(mla_g_q), row(mla_g_kv),
        wuq, wuk, wuv, invf, tm=min(INPROJ_ROWS, S))
    o_mla = _mla_attention(q, k, v)
    o_ml = _mlstm(mu, mv, mo, mg, ml_conv_w.astype(F32), _block_diag(ml_w_q).astype(BF16),
                  _block_diag(ml_w_k * (ML_DK ** -0.5)).T.astype(BF16), row(ml_g_head))
    mk, mvv = _memkv(mem, row(mem_g), mem_w_kv.astype(BF16))
    o_mem = _memattn(memq, mk, mvv, tq=min(MEMATTN_ROWS, S))

    T = B * S
    flat = lambda a: a.reshape(T, a.shape[-1])
    out = _out_block(flat(x), flat(gates), flat(o_mla), flat(o_ml), flat(o_mem),
                     w_branch.astype(BF16), w_out.astype(BF16), row(g_ffn), w_ffn_gate.astype(BF16),
                     w_ffn_up.astype(BF16), w_ffn_down.astype(BF16), row(g_final), tm=min(OUT_ROWS, T))
    return out.reshape(B, S, D)


def kernel(x, mem, positions, g_mix, w_in, mla_g_q, mla_g_kv, mla_w_uq, mla_w_uk, mla_w_uv, ml_conv_w,
           ml_w_q, ml_w_k, ml_gate_bias, ml_g_head, mem_g, mem_w_kv, w_branch, w_out, g_ffn,
           w_ffn_gate, w_ffn_up, w_ffn_down, g_final):
    depth = g_mix.shape[0]
    assert depth == 1, "the final norm is fused into the single layer's output kernel"
    return _layer(x, mem, positions, g_mix[0], w_in[0], mla_g_q[0], mla_g_kv[0], mla_w_uq[0],
                  mla_w_uk[0], mla_w_uv[0], ml_conv_w[0], ml_w_q[0], ml_w_k[0], ml_gate_bias[0],
                  ml_g_head[0], mem_g[0], mem_w_kv[0], w_branch[0], w_out[0], g_ffn[0],
                  w_ffn_gate[0], w_ffn_up[0], w_ffn_down[0], g_final)
```

```python
import math

import numpy as np
import jax
import jax.numpy as jnp
from jax import lax
from jax.experimental import pallas as pl
from jax.experimental.pallas import tpu as pltpu

D_MODEL = 1024
MEM_LEN = 256
EPS = 1e-6
MLA_HEADS = 8
MLA_NOPE = 64
MLA_ROPE = 32
MLA_V = 64
MLA_Q_RANK = 384
MLA_KV_RANK = 256
ROPE_THETA = 10000.0
ML_HEADS = 8
ML_DK = 64
ML_DV = 64
ML_WIDTH = ML_HEADS * ML_DV
CONV_WIDTH = 5
MEM_HEADS = 4
MEM_HEAD_DIM = 128
MEM_WIDTH = MEM_HEADS * MEM_HEAD_DIM
N_BRANCH = 3
N_ML_GATES = 4 * ML_HEADS
D_FF = 2816

LANES = 128
MLA_UNIT_ROWS = 512
MLA_STEP_HEADS = 4
ML_CHUNK = LANES
ML_PAIRS = ML_HEADS // 2
ML_REPL = 6
ML_DIR_LANES = ML_REPL * ML_HEADS
NEG_INIT = -1e30
CONV_PAD = 8
VMEM_LIMIT = 56 * 1024 * 1024
INPROJ_ROWS = 512
OUT_ROWS = 512
MEMATTN_ROWS = 2048

BF16 = jnp.bfloat16
F32 = jnp.float32

_C_GATE = 0
_C_CQ = _C_GATE + N_BRANCH * D_MODEL
_C_CKV = _C_CQ + MLA_Q_RANK
_C_KR = _C_CKV + MLA_KV_RANK
_C_MU = _C_KR + LANES
_C_MV = _C_MU + ML_WIDTH
_C_MO = _C_MV + ML_WIDTH
_C_MEMQ = _C_MO + ML_WIDTH
_C_MGI = _C_MEMQ + MEM_WIDTH
_C_MGF = _C_MGI + LANES
_C_END = _C_MGF + LANES


def _const_spec(shape):
    nd = len(shape)
    return pl.BlockSpec(shape, lambda *_: (0,) * nd, pipeline_mode=pl.Buffered(1))


def _sigmoid(x):
    return 1.0 / (1.0 + jnp.exp(-x))


def _rms(x, g):
    return x * lax.rsqrt(jnp.mean(x * x, axis=-1, keepdims=True) + EPS) * g


def _dot(a, b):
    return jnp.dot(a, b, preferred_element_type=F32)


def _dot_nt(a, b):
    return lax.dot_general(a, b, (((1,), (1,)), ((), ())), preferred_element_type=F32)


def _inproj_kernel(x_ref, pos_ref, gmix_ref, w_ref, gbias_ref, gq_ref, gkv_ref, wuq_ref, wuk_ref,
                   wuv_ref, rope_ref,
                   gates_ref, q_ref, k_ref, v_ref, mu_ref, mv_ref, mo_ref, mg_ref, memq_ref):
    x = x_ref[0]
    tm = x.shape[0]
    h = _rms(x, gmix_ref[...]).astype(BF16)

    def proj(a, b):
        return _dot(h, w_ref[:, a:b])

    cqn = _rms(proj(_C_CQ, _C_CKV), gq_ref[...]).astype(BF16)
    ckvn = _rms(proj(_C_CKV, _C_KR), gkv_ref[...]).astype(BF16)
    qe = _dot(cqn, wuq_ref[...])
    ket = _dot_nt(wuk_ref[...], ckvn)
    ve = _dot(ckvn, wuv_ref[...])

    mu_ref[0] = proj(_C_MU, _C_MV)
    mv_ref[0] = proj(_C_MV, _C_MO).astype(BF16)
    mo_ref[0] = _sigmoid(proj(_C_MO, _C_MEMQ)).astype(BF16)
    memq_ref[0] = (proj(_C_MEMQ, _C_MGI) * (MEM_HEAD_DIM ** -0.5 * math.log2(math.e))).astype(BF16)
    mg_ref[0] = proj(_C_MGI, _C_END) + gbias_ref[...]

    lane = lax.broadcasted_iota(jnp.int32, (tm, LANES), 1)
    hm = tm // 2
    pos = pos_ref[0].astype(F32)
    pos2 = jnp.where(lax.broadcasted_iota(jnp.int32, (hm, LANES), 1) < MLA_NOPE, pos[:hm], pos[hm:])
    rope2 = jnp.cos(pos2 * rope_ref[0:1, :] - rope_ref[1:2, :])
    rope = jnp.concatenate([pltpu.roll(rope2, MLA_NOPE, 1), rope2], axis=0)
    kra = proj(_C_KR, _C_MU) * rope
    kr = jnp.where(lane < MLA_NOPE, 0.0,
                   kra + pltpu.roll(kra, MLA_ROPE, 1) + pltpu.roll(kra, LANES - MLA_ROPE, 1))
    q_scale = (MLA_NOPE + MLA_ROPE) ** -0.5 * math.log2(math.e)
    qmul = jnp.where(lane < MLA_NOPE, 1.0, rope) * q_scale
    krt = kr.T
    for hh in range(MLA_HEADS):
        sl = slice(hh * LANES, (hh + 1) * LANES)
        q_ref[0, hh] = (qe[:, sl] * qmul).astype(BF16)
        k_ref[0, hh] = (ket[sl, :] + krt).astype(BF16)
        v_ref[0, hh] = jnp.where(lane == (MLA_V if hh % 2 == 0 else 0), 1.0, ve[:, sl]).astype(BF16)

    gates_ref[0] = _sigmoid(proj(_C_GATE, _C_CQ)).astype(BF16)


def _inproj(x, pos, g_mix, w_main, gbias, g_q, g_kv, wuq, wuk, wuv, invf, tm):
    B, S, D = x.shape
    grid = (B, S // tm)
    tok = lambda w: pl.BlockSpec((1, tm, w), lambda b, i: (b, i, 0))
    head = pl.BlockSpec((1, MLA_HEADS, tm, LANES), lambda b, i: (b, 0, i, 0))
    head_t = pl.BlockSpec((1, MLA_HEADS, LANES, tm), lambda b, i: (b, 0, 0, i))
    out_shape = (
        jax.ShapeDtypeStruct((B, S, N_BRANCH * D_MODEL), BF16),
        jax.ShapeDtypeStruct((B, MLA_HEADS, S, LANES), BF16),
        jax.ShapeDtypeStruct((B, MLA_HEADS, LANES, S), BF16),
        jax.ShapeDtypeStruct((B, MLA_HEADS, S, LANES), BF16),
        jax.ShapeDtypeStruct((B, S, ML_WIDTH), F32),
        jax.ShapeDtypeStruct((B, S, ML_WIDTH), BF16),
        jax.ShapeDtypeStruct((B, S, ML_WIDTH), BF16),
        jax.ShapeDtypeStruct((B, S, 2 * LANES), F32),
        jax.ShapeDtypeStruct((B, S, MEM_WIDTH), BF16),
    )
    out_specs = (tok(N_BRANCH * D_MODEL), head, head_t, head, tok(ML_WIDTH), tok(ML_WIDTH),
                 tok(ML_WIDTH), tok(2 * LANES), tok(MEM_WIDTH))
    consts = (g_mix, w_main, gbias, g_q, g_kv, wuq, wuk, wuv, invf)
    in_specs = [tok(D), pl.BlockSpec((1, tm, 1), lambda b, i: (b, i, 0))] + [_const_spec(a.shape) for a in consts]
    return pl.pallas_call(
        _inproj_kernel, grid=grid, in_specs=in_specs, out_specs=out_specs, out_shape=out_shape,
        compiler_params=pltpu.CompilerParams(dimension_semantics=("parallel", "parallel"),
                                             vmem_limit_bytes=VMEM_LIMIT),
        name="inproj",
    )(x, pos, *consts)


def _mla_kernel(q_ref, kt_ref, v_ref, o_ref, s_buf, p_buf):
    S = q_ref.shape[2]
    tu = s_buf.shape[1]
    low = lax.broadcasted_iota(jnp.int32, (tu, LANES), 1) < MLA_V
    unit = 0
    for pair in range(MLA_STEP_HEADS // 2):
        for t in range(S // tu):
            rows = slice(t * tu, (t + 1) * tu)
            outs = []
            for j in range(2):
                hh = 2 * pair + j
                slot = unit % 2
                unit += 1
                s_buf[slot] = _dot(q_ref[0, hh, rows, :], kt_ref[0, hh])
                s = s_buf[slot]
                p_buf[slot] = jnp.exp2(s - jnp.max(s, axis=-1, keepdims=True)).astype(BF16)
                r = _dot(p_buf[slot], v_ref[0, hh])
                den = r[:, MLA_V:MLA_V + 1] if j == 0 else r[:, 0:1]
                outs.append(r / den)
            o_ref[0, rows, pair * LANES:(pair + 1) * LANES] = jnp.where(low, outs[0], outs[1]).astype(BF16)


def _mla_attention(q, kt, v):
    B, H, S, _ = q.shape
    hs = MLA_STEP_HEADS
    tu = min(MLA_UNIT_ROWS, S)
    heads = lambda shape: pl.BlockSpec((1, hs) + shape, lambda b, p: (b, p, 0, 0))
    return pl.pallas_call(
        _mla_kernel, grid=(B, H // hs),
        in_specs=[heads((S, LANES)), heads((LANES, S)), heads((S, LANES))],
        out_specs=pl.BlockSpec((1, S, hs * MLA_V), lambda b, p: (b, 0, p)),
        out_shape=jax.ShapeDtypeStruct((B, S, H * MLA_V), BF16),
        scratch_shapes=[pltpu.VMEM((2, tu, S), F32),
                        pltpu.VMEM((2, tu, S), BF16)],
        compiler_params=pltpu.CompilerParams(dimension_semantics=("parallel", "parallel"),
                                             vmem_limit_bytes=VMEM_LIMIT),
        name="mla_attn",
    )(q, kt, v)


def _gate_lane(d, r, h):
    return ML_DIR_LANES * d + ML_HEADS * r + h


def _split3(x):
    hi = x.astype(BF16).astype(F32)
    r = x - hi
    mid = r.astype(BF16).astype(F32)
    lo = (r - mid).astype(BF16).astype(F32)
    return hi, mid, lo


def _pieces(srcs, rep):
    out = None
    for i, src in enumerate(srcs):
        for j, piece in enumerate(_split3(src)):
            out = piece if out is None else jnp.where(rep == 3 * i + j, piece, out)
    return out.astype(BF16)


def _cummax_dirs(x, fwd_lane):
    n = x.shape[0]
    row = lax.broadcasted_iota(jnp.int32, x.shape, 0)
    sh = 1
    while sh < n:
        if sh % 8 == 0:
            pad = jnp.full((sh, x.shape[1]), -jnp.inf, x.dtype)
            up = jnp.concatenate([pad, x[:n - sh]], axis=0)
            dn = jnp.concatenate([x[sh:], pad], axis=0)
        else:
            up = jnp.where(row >= sh, pltpu.roll(x, sh, 0), -jnp.inf)
            dn = jnp.where(row < n - sh, pltpu.roll(x, n - sh, 0), -jnp.inf)
        x = jnp.maximum(x, jnp.where(fwd_lane, up, dn))
        sh *= 2
    return x


def _mlstm_selectors():
    sel_a = np.zeros((LANES, 2, ML_PAIRS, 4 * LANES), np.float32)
    sel_m = np.zeros((LANES, 2, ML_PAIRS, LANES), np.float32)
    sel_d = np.zeros((LANES, 2, ML_PAIRS, 4 * LANES), np.float32)
    for d in range(2):
        for p in range(ML_PAIRS):
            for j in range(2):
                h = 2 * p + j
                for r in range(3):
                    a = _gate_lane(d, r, h)
                    g = _gate_lane(d, r + 3, h)
                    sel_a[a, d, p, j * LANES:(j + 1) * LANES] = 1
                    sel_a[a, d, p, 2 * LANES + j * ML_DK:2 * LANES + (j + 1) * ML_DK] = 1
                    sel_a[g, d, p, 3 * LANES + j * ML_DV:3 * LANES + (j + 1) * ML_DV] = 1
                    sel_m[a, d, p, j * ML_DK:(j + 1) * ML_DK] = 1
                    sel_d[a, d, p, j * 2 * LANES:(j + 1) * 2 * LANES] = 1
    as_bf = lambda m: jnp.asarray(m.reshape(LANES, -1), BF16)
    return as_bf(sel_a), as_bf(sel_m), as_bf(sel_d)


def _mlstm_kernel(u_ref, v_ref, so_ref, g_ref, cw_ref, wq_ref, wkt_ref, ghead_ref, sela_ref, selm_ref,
                  seld_ref, o_ref,
                  upad, h_s, q_s, kt_s, gate_s, a_s, ct_s, wst_s, rows_s, mpb_s, dec_s, c_s):
    S = u_ref.shape[1]
    L = ML_CHUNK
    nc = S // L
    half = CONV_WIDTH // 2

    zpad = jnp.zeros((CONV_PAD, ML_WIDTH), F32)
    upad[0:CONV_PAD, :] = zpad
    upad[CONV_PAD + S:CONV_PAD + S + CONV_PAD, :] = zpad
    upad[CONV_PAD:CONV_PAD + S, :] = u_ref[0]
    c_s[...] = jnp.zeros(c_s.shape, F32)
    rows_s[...] = jnp.zeros(rows_s.shape, F32)

    def conv_chunk(r):
        acc = jnp.zeros((L, ML_WIDTH), F32)
        for w in range(CONV_WIDTH):
            start = CONV_PAD - half + w + r * L
            acc = acc + upad[start:start + L, :] * cw_ref[w:w + 1, :]
        c = (acc * _sigmoid(acc)).astype(BF16)
        q_s[r * L:(r + 1) * L, :] = _dot(c, wq_ref[...]).astype(BF16)
        kt_s[r] = _dot_nt(wkt_ref[...], c).astype(BF16)

    def gate_lanes(nrows):
        lane = lax.broadcasted_iota(jnp.int32, (nrows, LANES), 1)
        fwd = lane < ML_DIR_LANES
        return fwd, jnp.right_shift(jnp.where(fwd, lane, lane - ML_DIR_LANES), 3)

    fwd_lane, rep = gate_lanes(L)
    fwd_row, _ = gate_lanes(1)
    tri_l = jnp.where(lax.broadcasted_iota(jnp.int32, (L, L), 0) >= lax.broadcasted_iota(jnp.int32, (L, L), 1),
                      1.0, 0.0).astype(BF16)
    btot, gmax = [], []
    for c in range(nc):
        rs = slice(c * L, (c + 1) * L)
        gf = g_ref[0, rs, LANES:2 * LANES]
        lf = jnp.minimum(gf, 0.0) - jnp.log1p(jnp.exp(-jnp.abs(gf)))
        bf = sum(_dot(tri_l, piece.astype(BF16)) for piece in _split3(lf))
        bt = bf[L - 1:L, :]
        bcum = jnp.where(fwd_lane, bf, bt - bf + lf)
        cc = g_ref[0, rs, 0:LANES] - bcum
        cmx = _cummax_dirs(cc, fwd_lane)
        gate_s[0, rs, :] = bcum
        gate_s[1, rs, :] = cc
        gate_s[2, rs, :] = cmx
        btot.append(bt)
        gmax.append(jnp.where(fwd_row, cmx[L - 1:L, :], cmx[0:1, :]))

    def scan_m(order):
        prev, new = [None] * nc, [None] * nc
        m = jnp.full((1, LANES), NEG_INIT, F32)
        for c in order:
            prev[c] = m
            m = jnp.maximum(btot[c] + m, btot[c] + gmax[c])
            new[c] = m
        return prev, new

    prev_f, new_f = scan_m(range(nc))
    prev_b, new_b = scan_m(reversed(range(nc)))

    for c in range(nc):
        rs = slice(c * L, (c + 1) * L)
        mprev = jnp.where(fwd_row, prev_f[c], prev_b[c])
        mnew = jnp.where(fwd_row, new_f[c], new_b[c])
        bcum, cc, cmx = gate_s[0, rs, :], gate_s[1, rs, :], gate_s[2, rs, :]
        mstab = jnp.maximum(mprev, cmx)
        a_s[rs, :] = _pieces((-mstab, -(bcum + mstab)), rep)
        ct_s[c] = cc.T
        wst_s[c] = jnp.exp(btot[c] + cc - mnew).T
        rows_s[0, c:c + 1, :] = mprev
        rows_s[1, c:c + 1, :] = jnp.exp(btot[c] + mprev - mnew)
    _, rep_c = gate_lanes(rows_s.shape[1])
    mpb = _dot(_pieces((rows_s[0],), rep_c), selm_ref[...])
    dec = _dot(_pieces((rows_s[1],), rep_c), seld_ref[...])
    for c in range(nc):
        mpb_s[c] = mpb[c:c + 1, :]
        dec_s[c] = dec[c:c + 1, :]

    t_i = lax.broadcasted_iota(jnp.int32, (L, L), 0)
    s_i = lax.broadcasted_iota(jnp.int32, (L, L), 1)
    low = lax.broadcasted_iota(jnp.int32, (L, LANES), 1) < ML_DV
    ones_e = jnp.where(low, 1.0, 0.0).astype(BF16)
    ones_o = jnp.where(low, 0.0, 1.0).astype(BF16)
    blk = 4 * LANES

    def finish(ci, p, hp):
        rows = slice(ci * L, (ci + 1) * L)
        sl = slice(p * LANES, (p + 1) * LANES)
        sq = hp * hp
        s_e = jnp.sum(jnp.where(low, sq, 0.0), axis=-1, keepdims=True)
        s_o = jnp.sum(jnp.where(low, 0.0, sq), axis=-1, keepdims=True)
        ms = jnp.where(low, s_e, s_o) * (1.0 / ML_DV)
        out = hp * lax.rsqrt(ms + EPS) * ghead_ref[:, sl] * so_ref[0, rows, sl].astype(F32)
        o_ref[0, rows, sl] = out.astype(BF16)

    def chunk_dir(ci, d, first):
        rows = slice(ci * L, (ci + 1) * L)
        allowed = t_i >= s_i if d == 0 else t_i <= s_i
        z = _dot(a_s[rows, :], sela_ref[:, d * ML_PAIRS * blk:(d + 1) * ML_PAIRS * blk])
        for p in range(ML_PAIRS):
            le = _gate_lane(d, 0, 2 * p)
            lo = le + 1
            dp = d * ML_PAIRS + p
            sl = slice(p * LANES, (p + 1) * LANES)
            zb = z[:, p * blk:(p + 1) * blk]
            qp = q_s[rows, sl]
            ktp = kt_s[ci, sl, :]
            vp = v_ref[0, rows, sl]
            zk = jnp.zeros((ML_DK, L), BF16)
            yt = jnp.concatenate([jnp.concatenate([ktp[:ML_DK], zk], axis=0),
                                  jnp.concatenate([zk, ktp[ML_DK:]], axis=0)], axis=1)
            qk = _dot(qp, yt)
            e_e = jnp.where(allowed, zb[:, 0:LANES] + ct_s[ci, le:le + 1, :], -jnp.inf)
            e_o = jnp.where(allowed, zb[:, LANES:2 * LANES] + ct_s[ci, lo:lo + 1, :], -jnp.inf)
            e_q = zb[:, 2 * LANES:3 * LANES] + mpb_s[ci, :, dp * LANES:(dp + 1) * LANES]
            pw = jnp.exp(jnp.concatenate([e_e, e_o, e_q], axis=1))
            x = (pw * jnp.concatenate([qk, qp.astype(F32)], axis=1)).astype(BF16)
            zero = jnp.zeros_like(vp)
            v2 = jnp.concatenate([jnp.concatenate([jnp.where(low, vp, zero), ones_e], axis=1),
                                  jnp.concatenate([jnp.where(low, zero, vp), ones_o], axis=1)], axis=0)
            cst = c_s[d, p]
            r = _dot(x, jnp.concatenate([v2, cst.astype(BF16)], axis=0))
            clamp = jnp.exp(zb[:, 3 * LANES:4 * LANES])
            hdir = r[:, 0:LANES] / jnp.maximum(jnp.abs(r[:, LANES:2 * LANES]), clamp)
            if first:
                h_s[rows, sl] = hdir
            else:
                finish(ci, p, h_s[rows, sl] + hdir)
            wsrow = jnp.concatenate([wst_s[ci, le:le + 1, :], wst_s[ci, lo:lo + 1, :]], axis=1)
            upd = _dot((yt.astype(F32) * wsrow).astype(BF16), v2)
            drow = dec_s[ci, :, dp * blk:(dp + 1) * blk]
            dmat = jnp.concatenate([jnp.broadcast_to(drow[:, 0:2 * LANES], (ML_DK, 2 * LANES)),
                                    jnp.broadcast_to(drow[:, 2 * LANES:4 * LANES], (ML_DK, 2 * LANES))], axis=0)
            c_s[d, p] = cst * dmat + upd

    conv_chunk(0)
    conv_chunk(nc - 1)
    for j in range(nc):
        first = j < nc - 1 - j
        chunk_dir(j, 0, first)
        chunk_dir(nc - 1 - j, 1, first)
        if j + 1 < nc - 2 - j:
            conv_chunk(j + 1)
            conv_chunk(nc - 2 - j)


def _mlstm(u, v, so, g, conv_w, wq_bd, wkt_bd, g_head):
    B, S, W = u.shape
    L = ML_CHUNK
    nc = S // L
    sels = _mlstm_selectors()
    seq = lambda w: pl.BlockSpec((1, S, w), lambda b: (b, 0, 0))
    consts = (conv_w, wq_bd, wkt_bd, g_head) + sels
    return pl.pallas_call(
        _mlstm_kernel, grid=(B,),
        in_specs=[seq(W), seq(W), seq(W), seq(2 * LANES)] + [_const_spec(a.shape) for a in consts],
        out_specs=seq(W),
        out_shape=jax.ShapeDtypeStruct((B, S, W), BF16),
        scratch_shapes=[
            pltpu.VMEM((S + 2 * CONV_PAD, W), F32),
            pltpu.VMEM((S, W), F32),
            pltpu.VMEM((S, W), BF16),
            pltpu.VMEM((nc, W, L), BF16),
            pltpu.VMEM((3, S, LANES), F32),
            pltpu.VMEM((S, LANES), BF16),
            pltpu.VMEM((nc, LANES, L), F32),
            pltpu.VMEM((nc, LANES, L), F32),
            pltpu.VMEM((2, -(-nc // 16) * 16, LANES), F32),
            pltpu.VMEM((nc, 1, 2 * ML_PAIRS * LANES), F32),
            pltpu.VMEM((nc, 1, 2 * ML_PAIRS * 4 * LANES), F32),
            pltpu.VMEM((2, ML_PAIRS, LANES, 2 * LANES), F32),
        ],
        compiler_params=pltpu.CompilerParams(dimension_semantics=("parallel",),
                                             vmem_limit_bytes=VMEM_LIMIT),
        name="mlstm",
    )(u, v, so, g, *consts)


def _memkv_kernel(mem_ref, g_ref, w_ref, k_ref, v_ref):
    mn = _rms(mem_ref[0], g_ref[...]).astype(BF16)
    kv = _dot(mn, w_ref[...])
    k_ref[0] = kv[:, :MEM_WIDTH].astype(BF16)
    v_ref[0] = kv[:, MEM_WIDTH:].astype(BF16)


def _memkv(mem, mem_g, w_kv):
    B, M, D = mem.shape
    blk = lambda w: pl.BlockSpec((1, M, w), lambda b: (b, 0, 0))
    return pl.pallas_call(
        _memkv_kernel, grid=(B,),
        in_specs=[blk(D), _const_spec(mem_g.shape), _const_spec(w_kv.shape)],
        out_specs=(blk(MEM_WIDTH), blk(MEM_WIDTH)),
        out_shape=(jax.ShapeDtypeStruct((B, M, MEM_WIDTH), BF16),) * 2,
        compiler_params=pltpu.CompilerParams(dimension_semantics=("parallel",)),
        name="memkv",
    )(mem, mem_g, w_kv)


def _memattn_kernel(q_ref, k_ref, v_ref, o_ref):
    for hh in range(MEM_HEADS):
        sl = slice(hh * MEM_HEAD_DIM, (hh + 1) * MEM_HEAD_DIM)
        s = _dot_nt(q_ref[0, :, sl], k_ref[0, :, sl])
        e = jnp.exp2(s - jnp.max(s, axis=-1, keepdims=True))
        den = jnp.sum(e, axis=-1, keepdims=True)
        o_ref[0, :, sl] = (_dot(e.astype(BF16), v_ref[0, :, sl]) / den).astype(BF16)


def _memattn(q, k, v, tq):
    B, S, W = q.shape
    M = k.shape[1]
    return pl.pallas_call(
        _memattn_kernel, grid=(B, S // tq),
        in_specs=[pl.BlockSpec((1, tq, W), lambda b, i: (b, i, 0)),
                  pl.BlockSpec((1, M, W), lambda b, i: (b, 0, 0)),
                  pl.BlockSpec((1, M, W), lambda b, i: (b, 0, 0))],
        out_specs=pl.BlockSpec((1, tq, W), lambda b, i: (b, i, 0)),
        out_shape=jax.ShapeDtypeStruct((B, S, W), BF16),
        compiler_params=pltpu.CompilerParams(dimension_semantics=("parallel", "parallel")),
        name="memattn",
    )(q, k, v)


def _out_kernel(x_ref, gates_ref, omla_ref, oml_ref, omem_ref, wb_ref, wout_ref, gffn_ref,
                wg_ref, wu_ref, wd_ref, gfin_ref, o_ref):
    merged = None
    for b, oref in enumerate((omla_ref, oml_ref, omem_ref)):
        y = _dot(oref[...], wb_ref[b]) * gates_ref[:, b * D_MODEL:(b + 1) * D_MODEL].astype(F32)
        merged = y if merged is None else merged + y
    x1 = x_ref[...] + _dot(merged.astype(BF16), wout_ref[...])
    h2 = _rms(x1, gffn_ref[...]).astype(BF16)
    hg = _dot(h2, wg_ref[...])
    a = (hg * _sigmoid(hg) * _dot(h2, wu_ref[...])).astype(BF16)
    x2 = x1 + _dot(a, wd_ref[...])
    o_ref[...] = _rms(x2, gfin_ref[...])


def _out_block(x, gates, o_mla, o_ml, o_mem, w_branch, w_out, g_ffn, w_g, w_u, w_d, g_final, tm):
    T, D = x.shape
    tok = lambda w: pl.BlockSpec((tm, w), lambda i: (i, 0))
    consts = (w_branch, w_out, g_ffn, w_g, w_u, w_d, g_final)
    return pl.pallas_call(
        _out_kernel, grid=(T // tm,),
        in_specs=[tok(D), tok(N_BRANCH * D), tok(o_mla.shape[1]), tok(o_ml.shape[1]), tok(o_mem.shape[1])]
        + [_const_spec(a.shape) for a in consts],
        out_specs=tok(D),
        out_shape=jax.ShapeDtypeStruct((T, D), x.dtype),
        compiler_params=pltpu.CompilerParams(dimension_semantics=("parallel",),
                                             vmem_limit_bytes=VMEM_LIMIT),
        name="out_block",
    )(x, gates, o_mla, o_ml, o_mem, *consts)


def _prep_in_weights(w_in, gate_bias):
    sizes = (N_BRANCH * D_MODEL, MLA_Q_RANK, MLA_KV_RANK, MLA_ROPE, ML_WIDTH, ML_WIDTH, ML_WIDTH,
             N_ML_GATES, MEM_WIDTH)
    offs = np.cumsum((0,) + sizes)
    w_gate, w_cq, w_ckv, w_kr, w_mu, w_mv, w_mo, w_mg, w_memq = [
        w_in[:, offs[i]:offs[i + 1]] for i in range(len(sizes))]
    half = MLA_ROPE // 2
    y1, y2 = w_kr[:, :half], w_kr[:, half:]
    z = lambda n: jnp.zeros((D_MODEL, n), w_in.dtype)
    w_kr = jnp.concatenate([z(MLA_NOPE), y1, y2, -y2, y1], axis=1)
    H = ML_HEADS
    pad = LANES - 2 * ML_DIR_LANES

    def gate_block(cols, fwd, bwd, zeros):
        return jnp.concatenate([cols(fwd)] * ML_REPL + [cols(bwd)] * ML_REPL + [zeros], axis=-1)

    wcols = lambda g: w_mg[:, g * H:(g + 1) * H]
    w_gi = gate_block(wcols, 0, 2, z(pad))
    w_gf = gate_block(wcols, 1, 3, z(pad))
    w_main = jnp.concatenate([w_gate, w_cq, w_ckv, w_kr, w_mu, w_mv, w_mo, w_memq, w_gi, w_gf],
                             axis=1).astype(BF16)
    bcols = lambda g: gate_bias[g]
    zb = jnp.zeros((pad,), gate_bias.dtype)
    gbias = jnp.concatenate([gate_block(bcols, 0, 2, zb), gate_block(bcols, 1, 3, zb)])[None, :]
    return w_main, gbias.astype(F32)


def _prep_mla_weights(w_uq, w_uk, w_uv):
    half = MLA_ROPE // 2
    qn = w_uq[:, :, :MLA_NOPE]
    x1 = w_uq[:, :, MLA_NOPE:MLA_NOPE + half]
    x2 = w_uq[:, :, MLA_NOPE + half:]
    wuq = jnp.concatenate([qn, x1, x2, -x2, x1], axis=-1).reshape(MLA_Q_RANK, MLA_HEADS * LANES)
    wuk = jnp.concatenate([w_uk, jnp.zeros_like(w_uk)], axis=-1).reshape(MLA_KV_RANK, MLA_HEADS * LANES).T
    wv = w_uv.reshape(MLA_KV_RANK, MLA_HEADS // 2, 2, MLA_V)
    zv = jnp.zeros_like(wv[:, :, 0])
    wuv = jnp.concatenate([wv[:, :, 0], zv, zv, wv[:, :, 1]], axis=-1).reshape(MLA_KV_RANK, MLA_HEADS * LANES)
    return wuq.astype(BF16), wuk.astype(BF16), wuv.astype(BF16)


def _block_diag(w):
    H, d, _ = w.shape
    eye = jnp.eye(H, dtype=w.dtype)
    return (eye[:, None, :, None] * w[:, :, None, :]).reshape(H * d, H * d)


def _layer(x, mem, positions, g_mix, w_in, mla_g_q, mla_g_kv, mla_w_uq, mla_w_uk, mla_w_uv,
           ml_conv_w, ml_w_q, ml_w_k, ml_gate_bias, ml_g_head, mem_g, mem_w_kv,
           w_branch, w_out, g_ffn, w_ffn_gate, w_ffn_up, w_ffn_down, g_final):
    B, S, D = x.shape
    row = lambda g: g.reshape(1, -1).astype(F32)
    w_main, gbias = _prep_in_weights(w_in, ml_gate_bias)
    wuq, wuk, wuv = _prep_mla_weights(mla_w_uq, mla_w_uk, mla_w_uv)
    half = MLA_ROPE // 2
    inv = ROPE_THETA ** (-jnp.arange(half, dtype=F32) / half)
    phase = np.where(np.arange(LANES) % MLA_NOPE < MLA_ROPE, 0.0, 0.5 * np.pi).astype(np.float32)
    invf = jnp.stack([jnp.tile(inv, LANES // half), jnp.asarray(phase)])

    gates, q, k, v, mu, mv, mo, mg, memq = _inproj(
        x, positions.reshape(B, S, 1), row(g_mix), w_main, gbias, row(mla_g_q), row(mla_g_kv),
        wuq, wuk, wuv, invf, tm=min(INPROJ_ROWS, S))
    o_mla = _mla_attention(q, k, v)
    o_ml = _mlstm(mu, mv, mo, mg, ml_conv_w.astype(F32), _block_diag(ml_w_q).astype(BF16),
                  _block_diag(ml_w_k * (ML_DK ** -0.5)).T.astype(BF16), row(ml_g_head))
    mk, mvv = _memkv(mem, row(mem_g), mem_w_kv.astype(BF16))
    o_mem = _memattn(memq, mk, mvv, tq=min(MEMATTN_ROWS, S))

    T = B * S
    flat = lambda a: a.reshape(T, a.shape[-1])
    out = _out_block(flat(x), flat(gates), flat(o_mla), flat(o_ml), flat(o_mem),
                     w_branch.astype(BF16), w_out.astype(BF16), row(g_ffn), w_ffn_gate.astype(BF16),
                     w_ffn_up.astype(BF16), w_ffn_down.astype(BF16), row(g_final), tm=min(OUT_ROWS, T))
    return out.reshape(B, S, D)


def kernel(x, mem, positions, g_mix, w_in, mla_g_q, mla_g_kv, mla_w_uq, mla_w_uk, mla_w_uv, ml_conv_w,
           ml_w_q, ml_w_k, ml_gate_bias, ml_g_head, mem_g, mem_w_kv, w_branch, w_out, g_ffn,
           w_ffn_gate, w_ffn_up, w_ffn_down, g_final):
    depth = g_mix.shape[0]
    assert depth == 1, "the final norm is fused into the single layer's output kernel"
    return _layer(x, mem, positions, g_mix[0], w_in[0], mla_g_q[0], mla_g_kv[0], mla_w_uq[0],
                  mla_w_uk[0], mla_w_uv[0], ml_conv_w[0], ml_w_q[0], ml_w_k[0], ml_gate_bias[0],
                  ml_g_head[0], mem_g[0], mem_w_kv[0], w_branch[0], w_out[0], g_ffn[0],
                  w_ffn_gate[0], w_ffn_up[0], w_ffn_down[0], g_final)
```
